```python
import math
import jax
import jax.numpy as jnp
from jax import lax
import numpy as np

D_MODEL = 1024
BATCH = 8
SEQ = 2048
DEPTH = 2
DEC_BATCH = 128
DEC_SEQ = 4
PAST_LEN = 16384
PAGE_SIZE = 128

MIX_WIDTH = D_MODEL
N_MIXERS = 4
GROUP_WIDTH = MIX_WIDTH // N_MIXERS
CONV_W = 4
SSD_HEADDIM = 64
SSD_HEADS = GROUP_WIDTH // SSD_HEADDIM
SSD_BC_GROUPS = 2
SSD_STATE = 128
SSD_CONV_DIM = GROUP_WIDTH + 2 * SSD_BC_GROUPS * SSD_STATE
SSD_CHUNK = 64
S5_GROUP_CH = 16
S5_GROUPS = GROUP_WIDTH // S5_GROUP_CH
S5_STATE = 64
GDN_HEADDIM = 64
GDN_HEADS = GROUP_WIDTH // GDN_HEADDIM
GDN_CONV_DIM = 3 * GROUP_WIDTH
GDN_CHUNK = 64
GLA_HEADS = 4
GLA_DV = GROUP_WIDTH // GLA_HEADS
GLA_DK = GLA_DV // 2
GLA_GATE_RANK = 16
GLA_TAU = 16.0
GLA_CHUNK = 16
D_FF = 256 * ((8 * D_MODEL // 3 + 255) // 256)
FFN_CONV_W = 3
EPS = 1e-6
IN_SPLITS = (GROUP_WIDTH, SSD_CONV_DIM, SSD_HEADS,
             GROUP_WIDTH,
             GDN_CONV_DIM, GROUP_WIDTH, GDN_HEADS, GDN_HEADS,
             GLA_HEADS * GLA_DK, GLA_HEADS * GLA_DK, GROUP_WIDTH, GROUP_WIDTH, GLA_GATE_RANK)
N_IN = sum(IN_SPLITS)

kernel_name = 'hybrid_ssd_s5_gdn_gla_convffn_step'


def rmsnorm(x, g):
    x = x.astype(jnp.float32)
    return x * lax.rsqrt(jnp.mean(x * x, axis=-1, keepdims=True) + EPS) * g


def l2norm(x):
    return x * lax.rsqrt(jnp.sum(x * x, axis=-1, keepdims=True) + EPS)


def causal_dwconv(x, buf, w):
    width, seq = w.shape[0], x.shape[1]
    xp = jnp.concatenate([buf.astype(x.dtype), x], axis=1)
    y = sum(xp[:, k:k + seq] * w[k] for k in range(width))
    return y, xp[:, seq:]


def to_chunks(x, c):
    x = jnp.moveaxis(x, 2, 1)
    return x.reshape(x.shape[:2] + (x.shape[2] // c, c) + x.shape[3:])


def from_chunks(x):
    x = x.reshape(x.shape[:2] + (x.shape[2] * x.shape[3],) + x.shape[4:])
    return jnp.moveaxis(x, 1, 2)


def decay_matrix(cum, mask):
    seg = cum[..., :, None] - cum[..., None, :]
    return jnp.exp(jnp.where(mask, seg, -jnp.inf))


def inter_chunk(s0, dec, upd):
    def step(s, inp):
        d, u = inp
        return s * d + u, s
    s_fin, starts = lax.scan(step, s0, (jnp.moveaxis(dec, 2, 0), jnp.moveaxis(upd, 2, 0)))
    return jnp.moveaxis(starts, 0, 2), s_fin


def ssd_scan(xh, dt, a, bm, cm, h0):
    c = math.gcd(xh.shape[1], SSD_CHUNK)
    rep = SSD_HEADS // SSD_BC_GROUPS
    mask = jnp.tril(jnp.ones((c, c), dtype=bool))
    xq = to_chunks(xh * dt[..., None], c)
    bq = to_chunks(jnp.repeat(bm, rep, axis=2), c)
    cq = to_chunks(jnp.repeat(cm, rep, axis=2), c)
    cum = jnp.cumsum(to_chunks(dt * a, c), axis=-1)
    scores = jnp.einsum('bhqtn,bhqsn->bhqts', cq, bq) * decay_matrix(cum, mask)
    y_intra = jnp.einsum('bhqts,bhqsp->bhqtp', scores, xq)
    tail = jnp.exp(cum[..., -1:] - cum)
    upd = jnp.einsum('bhqsn,bhqsp->bhqnp', bq * tail[..., None], xq)
    dec = jnp.exp(cum[..., -1])[..., None, None]
    starts, h_fin = inter_chunk(h0, dec, upd)
    y_inter = jnp.einsum('bhqtn,bhqnp->bhqtp', cq * jnp.exp(cum)[..., None], starts)
    return from_chunks(y_intra + y_inter), h_fin


def gdn_scan(q, k, v, g, beta, s0):
    c = math.gcd(q.shape[1], GDN_CHUNK)
    incl = jnp.tril(jnp.ones((c, c), dtype=bool))
    strict = jnp.tril(jnp.ones((c, c), dtype=bool), -1)
    qq, kq, vq = to_chunks(q, c), to_chunks(k, c), to_chunks(v, c)
    bq = to_chunks(beta, c)
    cum = jnp.cumsum(to_chunks(g, c), axis=-1)
    dec = decay_matrix(cum, incl)
    a_mat = jnp.where(strict, dec * jnp.einsum('bhqtd,bhqsd->bhqts', kq, kq), 0.0) * bq[..., None]
    rhs = jnp.concatenate([vq * bq[..., None], kq * (bq * jnp.exp(cum))[..., None]], axis=-1)
    eye = jnp.eye(c, dtype=a_mat.dtype)
    sol = lax.linalg.triangular_solve(a_mat + eye, rhs, left_side=True, lower=True, unit_diagonal=True)
    u_ch, w_ch = sol[..., :GDN_HEADDIM], sol[..., GDN_HEADDIM:]
    qk = jnp.einsum('bhqtd,bhqsd->bhqts', qq, kq) * dec
    gq = qq * jnp.exp(cum)[..., None]
    kt = kq * jnp.exp(cum[..., -1:] - cum)[..., None]
    cd = jnp.exp(cum[..., -1])[..., None, None]

    def step(s, inp):
        u_, w_, qk_, gq_, kt_, cd_ = inp
        delta = u_ - w_ @ s
        o = gq_ @ s + qk_ @ delta
        return s * cd_ + jnp.swapaxes(kt_, -1, -2) @ delta, o

    xs = (u_ch, w_ch, qk, gq, kt, cd)
    s_fin, o = lax.scan(step, s0, tuple(jnp.moveaxis(t, 2, 0) for t in xs))
    return from_chunks(jnp.moveaxis(o, 0, 2)), s_fin


def gla_scan(q, k, v, lg, s0):
    c = math.gcd(q.shape[1], GLA_CHUNK)
    mask = jnp.tril(jnp.ones((c, c), dtype=bool))
    qq, kq, vq = to_chunks(q, c), to_chunks(k, c), to_chunks(v, c)
    cum = jnp.cumsum(to_chunks(lg, c), axis=3)
    seg = cum[..., :, None, :] - cum[..., None, :, :]
    dec = jnp.exp(jnp.where(mask[:, :, None], seg, -jnp.inf))
    scores = jnp.einsum('bhqtd,bhqsd,bhqtsd->bhqts', qq, kq, dec)
    o_intra = scores @ vq
    kt = kq * jnp.exp(cum[..., -1:, :] - cum)
    upd = jnp.einsum('bhqsd,bhqsv->bhqdv', kt, vq)
    dec_c = jnp.exp(cum[..., -1, :])[..., None]
    starts, s_fin = inter_chunk(s0, dec_c, upd)
    o_inter = jnp.einsum('bhqtd,bhqdv->bhqtv', qq * jnp.exp(cum), starts)
    return from_chunks(o_intra + o_inter), s_fin


def complex_affine_combine(e1, e2):
    ar1, ai1, br1, bi1 = e1
    ar2, ai2, br2, bi2 = e2
    return (ar2 * ar1 - ai2 * ai1, ar2 * ai1 + ai2 * ar1,
            ar2 * br1 - ai2 * bi1 + br2, ar2 * bi1 + ai2 * br1 + bi2)


def ssd_mixer(z, xbc, dt_raw, h0, buf, p):
    bsz, seq, _ = z.shape
    xbc, new_buf = causal_dwconv(xbc, buf, p['ssd_conv_w'])
    xbc = jax.nn.silu(xbc + p['ssd_conv_b'])
    xs, bm, cm = jnp.split(xbc, [GROUP_WIDTH, GROUP_WIDTH + SSD_BC_GROUPS * SSD_STATE], axis=-1)
    xh = xs.reshape(bsz, seq, SSD_HEADS, SSD_HEADDIM)
    bm = bm.reshape(bsz, seq, SSD_BC_GROUPS, SSD_STATE)
    cm = cm.reshape(bsz, seq, SSD_BC_GROUPS, SSD_STATE)
    dt = jax.nn.softplus(dt_raw + p['ssd_dt_bias'])
    y, h_fin = ssd_scan(xh, dt, -jnp.exp(p['ssd_a_log']), bm, cm, h0)
    y = (y + p['ssd_d'][:, None] * xh).reshape(bsz, seq, GROUP_WIDTH)
    return rmsnorm(y * jax.nn.silu(z), p['ssd_norm']), h_fin, new_buf


def s5_mixer(u, h0_re, h0_im, p):
    f32 = jnp.float32
    bsz, seq, _ = u.shape
    a_re, a_im = p['s5_a_re'].astype(f32), p['s5_a_im'].astype(f32)
    dt = jnp.exp(p['s5_log_dt'].astype(f32))[:, None]
    mag = jnp.exp(dt * a_re)
    ab_re, ab_im = mag * jnp.cos(dt * a_im), mag * jnp.sin(dt * a_im)
    den = a_re * a_re + a_im * a_im
    f_re = ((ab_re - 1.0) * a_re + ab_im * a_im) / den
    f_im = (ab_im * a_re - (ab_re - 1.0) * a_im) / den
    b_re, b_im = p['s5_b_re'].astype(f32), p['s5_b_im'].astype(f32)
    bb_re = f_re[..., None] * b_re - f_im[..., None] * b_im
    bb_im = f_re[..., None] * b_im + f_im[..., None] * b_re
    ug = u.reshape(bsz, seq, S5_GROUPS, S5_GROUP_CH)
    bu_re = jnp.einsum('blgc,gnc->blgn', ug, bb_re)
    bu_im = jnp.einsum('blgc,gnc->blgn', ug, bb_im)
    bu_re = bu_re.at[:, 0].add(ab_re * h0_re - ab_im * h0_im)
    bu_im = bu_im.at[:, 0].add(ab_re * h0_im + ab_im * h0_re)
    ar = jnp.broadcast_to(ab_re, bu_re.shape)
    ai = jnp.broadcast_to(ab_im, bu_im.shape)
    _, _, h_re, h_im = lax.associative_scan(complex_affine_combine, (ar, ai, bu_re, bu_im), axis=1)
    c_re, c_im = p['s5_c_re'].astype(f32), p['s5_c_im'].astype(f32)
    y = jnp.einsum('blgn,gcn->blgc', h_re, c_re) - jnp.einsum('blgn,gcn->blgc', h_im, c_im)
    y = y.reshape(bsz, seq, GROUP_WIDTH) + p['s5_d'] * u
    ya, yb = jnp.split(y @ p['s5_glu_w'], 2, axis=-1)
    return rmsnorm(ya * jax.nn.sigmoid(yb), p['s5_norm']), h_re[:, -1], h_im[:, -1]


def gdn_mixer(qkv, z, a_raw, b_raw, s0, buf, p):
    bsz, seq, _ = z.shape
    qkv, new_buf = causal_dwconv(qkv, buf, p['gdn_conv_w'])
    q, k, v = [t.reshape(bsz, seq, GDN_HEADS, GDN_HEADDIM) for t in jnp.split(jax.nn.silu(qkv), 3, axis=-1)]
    q = l2norm(q) * GDN_HEADDIM ** -0.5
    k = l2norm(k)
    beta = jax.nn.sigmoid(b_raw)
    g = -jnp.exp(p['gdn_a_log']) * jax.nn.softplus(a_raw + p['gdn_dt_bias'])
    o, s_fin = gdn_scan(q, k, v, g, beta, s0)
    o = rmsnorm(o, p['gdn_norm']) * jax.nn.silu(z.reshape(bsz, seq, GDN_HEADS, GDN_HEADDIM))
    return o.reshape(bsz, seq, GROUP_WIDTH), s_fin, new_buf


def gla_mixer(q, k, v, gate, lr, s0, p):
    bsz, seq, _ = q.shape
    q = q.reshape(bsz, seq, GLA_HEADS, GLA_DK) * GLA_DK ** -0.5
    k = k.reshape(bsz, seq, GLA_HEADS, GLA_DK)
    v = v.reshape(bsz, seq, GLA_HEADS, GLA_DV)
    lg = jax.nn.log_sigmoid(lr @ p['gla_wg2'] + p['gla_bg']).reshape(bsz, seq, GLA_HEADS, GLA_DK) / GLA_TAU
    o, s_fin = gla_scan(q, k, v, lg, s0)
    o = rmsnorm(o, p['gla_norm']) * jax.nn.silu(gate.reshape(bsz, seq, GLA_HEADS, GLA_DV))
    return o.reshape(bsz, seq, GROUP_WIDTH), s_fin


def conv_ffn(h, buf, p):
    u, new_buf = causal_dwconv(h @ p['ffn_up'], buf, p['ffn_conv_w'])
    gate, val = jnp.split(u + p['ffn_conv_b'], 2, axis=-1)
    return (jax.nn.silu(gate) * val) @ p['ffn_down'], new_buf


def decoder_layer(x, c, st, p):
    ssd_h, ssd_buf, s5_re, s5_im, gdn_s, gdn_buf, gla_s, ffn_buf = st
    mod = jax.nn.silu(c) @ p['ada_w'] + p['ada_b']
    sh1, sc1, g1, sh2, sc2, g2 = jnp.split(mod[:, None, :], 6, axis=-1)
    h = rmsnorm(x, p['norm_mix']) * (1.0 + sc1) + sh1
    idx = np.cumsum(IN_SPLITS)[:-1].tolist()
    (ssd_z, ssd_xbc, ssd_dt, s5_u, gdn_qkv, gdn_z, gdn_a, gdn_b,
     gla_q, gla_k, gla_v, gla_g, gla_lr) = jnp.split(h @ p['w_in'], idx, axis=-1)
    y_ssd, ssd_h, ssd_buf = ssd_mixer(ssd_z, ssd_xbc, ssd_dt, ssd_h, ssd_buf, p)
    y_s5, s5_re, s5_im = s5_mixer(s5_u, s5_re, s5_im, p)
    y_gdn, gdn_s, gdn_buf = gdn_mixer(gdn_qkv, gdn_z, gdn_a, gdn_b, gdn_s, gdn_buf, p)
    y_gla, gla_s = gla_mixer(gla_q, gla_k, gla_v, gla_g, gla_lr, gla_s, p)
    mix = jnp.concatenate([y_ssd, y_s5, y_gdn, y_gla], axis=-1) @ p['w_out']
    x = x + g1 * mix
    h = rmsnorm(x, p['norm_ffn']) * (1.0 + sc2) + sh2
    f, ffn_buf = conv_ffn(h, ffn_buf, p)
    x = x + g2 * f
    return x, (ssd_h, ssd_buf, s5_re, s5_im, gdn_s, gdn_buf, gla_s, ffn_buf)


def zero_states(b):
    f32 = jnp.float32
    return (jnp.zeros((b, SSD_HEADS, SSD_STATE, SSD_HEADDIM), f32),
            jnp.zeros((b, CONV_W - 1, SSD_CONV_DIM), f32),
            jnp.zeros((b, S5_GROUPS, S5_STATE), f32),
            jnp.zeros((b, S5_GROUPS, S5_STATE), f32),
            jnp.zeros((b, GDN_HEADS, GDN_HEADDIM, GDN_HEADDIM), f32),
            jnp.zeros((b, CONV_W - 1, GDN_CONV_DIM), f32),
            jnp.zeros((b, GLA_HEADS, GLA_DK, GLA_DV), f32),
            jnp.zeros((b, FFN_CONV_W - 1, 2 * D_FF), f32))


def trunk(x, c, states, params, final_norm):
    x = x.astype(jnp.float32)
    c = c.astype(jnp.float32)
    new_states = []
    for l in range(DEPTH):
        x, ns = decoder_layer(x, c, states[l], params[l])
        new_states.append(ns)
    return rmsnorm(x, final_norm), new_states


def setup_inputs(seed: int = 0) -> dict:
    key = jax.random.key(seed)
    keys = iter(jax.random.split(key, 64))
    f32 = jnp.float32

    def nrm(shape, scale=1.0):
        return jax.random.normal(next(keys), shape, f32) * scale

    def unif(shape, lo, hi):
        return jax.random.uniform(next(keys), shape, f32, lo, hi)

    def gain(shape):
        return 1.0 + nrm(shape, 0.02)

    def dt_bias(shape):
        dt = jnp.exp(unif(shape, math.log(1e-3), math.log(1e-1)))
        return dt + jnp.log(-jnp.expm1(-dt))

    L = DEPTH
    return {
        'x_prompt': nrm((BATCH, SEQ, D_MODEL)),
        'x_sample': nrm((DEC_BATCH, DEC_SEQ, D_MODEL)),
        'c_prompt': nrm((BATCH, D_MODEL)),
        'c_sample': nrm((DEC_BATCH, D_MODEL)),
        'state_ssd': nrm((L, DEC_BATCH, SSD_HEADS, SSD_STATE, SSD_HEADDIM), 0.1),
        'state_ssd_conv': nrm((L, DEC_BATCH, CONV_W - 1, SSD_CONV_DIM)),
        'state_s5_re': nrm((L, DEC_BATCH, S5_GROUPS, S5_STATE), 0.1),
        'state_s5_im': nrm((L, DEC_BATCH, S5_GROUPS, S5_STATE), 0.1),
        'state_gdn': nrm((L, DEC_BATCH, GDN_HEADS, GDN_HEADDIM, GDN_HEADDIM), 0.1),
        'state_gdn_conv': nrm((L, DEC_BATCH, CONV_W - 1, GDN_CONV_DIM)),
        'state_gla': nrm((L, DEC_BATCH, GLA_HEADS, GLA_DK, GLA_DV), 0.1),
        'state_ffn_conv': nrm((L, DEC_BATCH, FFN_CONV_W - 1, 2 * D_FF)),
        'ada_w': nrm((L, D_MODEL, 6 * D_MODEL), 0.5 * D_MODEL ** -0.5),
        'ada_b': nrm((L, 6 * D_MODEL), 0.02),
        'norm_mix': gain((L, D_MODEL)),
        'norm_ffn': gain((L, D_MODEL)),
        'w_in': nrm((L, D_MODEL, N_IN), D_MODEL ** -0.5),
        'w_out': nrm((L, MIX_WIDTH, D_MODEL), MIX_WIDTH ** -0.5),
        'ssd_conv_w': nrm((L, CONV_W, SSD_CONV_DIM), CONV_W ** -0.5),
        'ssd_conv_b': nrm((L, SSD_CONV_DIM), 0.02),
        'ssd_dt_bias': dt_bias((L, SSD_HEADS)),
        'ssd_a_log': jnp.log(unif((L, SSD_HEADS), 1.0, 16.0)),
        'ssd_d': 1.0 + nrm((L, SSD_HEADS), 0.1),
        'ssd_norm': gain((L, GROUP_WIDTH)),
        's5_a_re': -0.5 + nrm((L, S5_GROUPS, S5_STATE), 0.01),
        's5_a_im': jnp.pi * jnp.arange(S5_STATE, dtype=f32) + nrm((L, S5_GROUPS, S5_STATE), 0.01),
        's5_log_dt': unif((L, S5_GROUPS), math.log(1e-3), math.log(1e-1)),
        's5_b_re': nrm((L, S5_GROUPS, S5_STATE, S5_GROUP_CH), (2 * S5_GROUP_CH) ** -0.5),
        's5_b_im': nrm((L, S5_GROUPS, S5_STATE, S5_GROUP_CH), (2 * S5_GROUP_CH) ** -0.5),
        's5_c_re': nrm((L, S5_GROUPS, S5_GROUP_CH, S5_STATE), (2 * S5_STATE) ** -0.5),
        's5_c_im': nrm((L, S5_GROUPS, S5_GROUP_CH, S5_STATE), (2 * S5_STATE) ** -0.5),
        's5_d': nrm((L, GROUP_WIDTH)),
        's5_glu_w': nrm((L, GROUP_WIDTH, 2 * GROUP_WIDTH), GROUP_WIDTH ** -0.5),
        's5_norm': gain((L, GROUP_WIDTH)),
        'gdn_conv_w': nrm((L, CONV_W, GDN_CONV_DIM), CONV_W ** -0.5),
        'gdn_a_log': jnp.log(unif((L, GDN_HEADS), 1.0, 16.0)),
        'gdn_dt_bias': dt_bias((L, GDN_HEADS)),
        'gdn_norm': gain((L, GDN_HEADDIM)),
        'gla_wg2': nrm((L, GLA_GATE_RANK, GLA_HEADS * GLA_DK), GLA_GATE_RANK ** -0.5),
        'gla_bg': nrm((L, GLA_HEADS * GLA_DK), 0.1),
        'gla_norm': gain((L, GLA_DV)),
        'ffn_up': nrm((L, D_MODEL, 2 * D_FF), D_MODEL ** -0.5),
        'ffn_conv_w': nrm((L, FFN_CONV_W, 2 * D_FF), FFN_CONV_W ** -0.5),
        'ffn_conv_b': nrm((L, 2 * D_FF), 0.02),
        'ffn_down': nrm((L, D_FF, D_MODEL), D_FF ** -0.5),
        'final_norm': gain((D_MODEL,)),
    }


def reference(x_prompt, x_sample, c_prompt, c_sample,
              state_ssd, state_ssd_conv, state_s5_re, state_s5_im,
              state_gdn, state_gdn_conv, state_gla, state_ffn_conv,
              ada_w, ada_b, norm_mix, norm_ffn, w_in, w_out,
              ssd_conv_w, ssd_conv_b, ssd_dt_bias, ssd_a_log, ssd_d, ssd_norm,
              s5_a_re, s5_a_im, s5_log_dt, s5_b_re, s5_b_im, s5_c_re, s5_c_im, s5_d, s5_glu_w, s5_norm,
              gdn_conv_w, gdn_a_log, gdn_dt_bias, gdn_norm,
              gla_wg2, gla_bg, gla_norm,
              ffn_up, ffn_conv_w, ffn_conv_b, ffn_down,
              final_norm):
    params = [dict(ada_w=ada_w[l], ada_b=ada_b[l], norm_mix=norm_mix[l], norm_ffn=norm_ffn[l],
                   w_in=w_in[l], w_out=w_out[l],
                   ssd_conv_w=ssd_conv_w[l], ssd_conv_b=ssd_conv_b[l], ssd_dt_bias=ssd_dt_bias[l],
                   ssd_a_log=ssd_a_log[l], ssd_d=ssd_d[l], ssd_norm=ssd_norm[l],
                   s5_a_re=s5_a_re[l], s5_a_im=s5_a_im[l], s5_log_dt=s5_log_dt[l],
                   s5_b_re=s5_b_re[l], s5_b_im=s5_b_im[l], s5_c_re=s5_c_re[l], s5_c_im=s5_c_im[l],
                   s5_d=s5_d[l], s5_glu_w=s5_glu_w[l], s5_norm=s5_norm[l],
                   gdn_conv_w=gdn_conv_w[l], gdn_a_log=gdn_a_log[l], gdn_dt_bias=gdn_dt_bias[l],
                   gdn_norm=gdn_norm[l],
                   gla_wg2=gla_wg2[l], gla_bg=gla_bg[l], gla_norm=gla_norm[l],
                   ffn_up=ffn_up[l], ffn_conv_w=ffn_conv_w[l], ffn_conv_b=ffn_conv_b[l],
                   ffn_down=ffn_down[l])
              for l in range(DEPTH)]
    prompt_states = [zero_states(x_prompt.shape[0]) for _ in range(DEPTH)]
    y_p, new_p = trunk(x_prompt, c_prompt, prompt_states, params, final_norm)
    cached = (state_ssd, state_ssd_conv, state_s5_re, state_s5_im,
              state_gdn, state_gdn_conv, state_gla, state_ffn_conv)
    sample_states = [[t[l].astype(jnp.float32) for t in cached] for l in range(DEPTH)]
    y_s, new_s = trunk(x_sample, c_sample, sample_states, params, final_norm)
    p_ssd, p_ssd_conv, p_s5_re, p_s5_im, p_gdn, p_gdn_conv, p_gla, p_ffn_conv = [jnp.stack(t) for t in zip(*new_p)]
    s_ssd, s_ssd_conv, s_s5_re, s_s5_im, s_gdn, s_gdn_conv, s_gla, s_ffn_conv = [jnp.stack(t) for t in zip(*new_s)]
    y_p = y_p.astype(x_prompt.dtype)
    y_s = y_s.astype(x_sample.dtype)
    return (y_p, y_s,
            p_ssd, p_ssd_conv, p_s5_re, p_s5_im, p_gdn, p_gdn_conv, p_gla, p_ffn_conv,
            s_ssd, s_ssd_conv, s_s5_re, s_s5_im, s_gdn, s_gdn_conv, s_gla, s_ffn_conv)
```

```python
import functools

import numpy as np
import jax
import jax.numpy as jnp
from jax import lax
from jax.experimental import pallas as pl
from jax.experimental.pallas import tpu as pltpu

F32 = jnp.float32
BF16 = jnp.bfloat16

D_MODEL = 1024
DEPTH = 2
GROUP_WIDTH = 256
CONV_W = 4
SSD_HEADS = 4
SSD_HEADDIM = 64
SSD_STATE = 128
SSD_CONV_DIM = 768
S5_GROUPS = 16
S5_GROUP_CH = 16
S5_STATE = 64
S5_WIDTH = S5_GROUPS * S5_STATE
GDN_HEADS = 4
GDN_HEADDIM = 64
GDN_CONV_DIM = 768
GLA_HEADS = 4
GLA_DK = 32
GLA_DV = 64
GLA_GATE_RANK = 16
GLA_TAU = 16.0
D_FF = 2816
FFN_CONV_W = 3
EPS = 1e-6

NZ = 3200
Z_XBC, Z_QKV, Z_Z, Z_U, Z_GZ, Z_V, Z_GG, Z_Q, Z_K, Z_SM = 0, 768, 1536, 1792, 2048, 2304, 2560, 2816, 2944, 3072
SM_DT, SM_GA, SM_GB, SM_LR = 0, 4, 8, 12

SUBLANES = 8
SLOT = SUBLANES
SLOT_OFF = SLOT - 4
SCAN_CHUNK = 64
GLA_CHUNK = 16
TILE_TOKENS = 256
SAMPLE_MIX_TILE = 64
SAMPLE_FFN_TILE = 128
FF_CHUNK = 256
VMEM_LIMIT = 56 * 1024 * 1024


def _silu(x):
    return x * (1.0 / (1.0 + jnp.exp(-x)))


def _sigmoid(x):
    return 1.0 / (1.0 + jnp.exp(-x))


def _softplus(x):
    return jnp.maximum(x, 0.0) + jnp.log(1.0 + jnp.exp(-jnp.abs(x)))


def _dot(a, b):
    return jnp.dot(a.astype(BF16), b.astype(BF16), preferred_element_type=F32)


def _dot_f32(a, b):
    return jnp.dot(a, b, preferred_element_type=F32, precision=lax.Precision.HIGHEST)


def _dot_nt(a, b):
    return lax.dot_general(a.astype(BF16), b.astype(BF16), (((1,), (1,)), ((), ())), preferred_element_type=F32)


def _dot_tn(a, b):
    return lax.dot_general(a.astype(BF16), b.astype(BF16), (((0,), (0,)), ((), ())), preferred_element_type=F32)


def _split_hi_lo(x):
    hi = x.astype(BF16)
    lo = (x - hi.astype(F32)).astype(BF16)
    return hi, lo


def _dot_exact01(m01, x):
    hi, lo = _split_hi_lo(x)
    return jnp.dot(m01, hi, preferred_element_type=F32) + jnp.dot(m01, lo, preferred_element_type=F32)


def _dot_exact01_tn(x, m01):
    hi, lo = _split_hi_lo(x)
    dn = (((0,), (0,)), ((), ()))
    return (lax.dot_general(hi, m01, dn, preferred_element_type=F32)
            + lax.dot_general(lo, m01, dn, preferred_element_type=F32))


def _dot_x_exact01(x, m01):
    hi, lo = _split_hi_lo(x)
    return jnp.dot(hi, m01, preferred_element_type=F32) + jnp.dot(lo, m01, preferred_element_type=F32)


def _rms(x, gain):
    return x * lax.rsqrt(jnp.mean(x * x, axis=-1, keepdims=True) + EPS) * gain


def _expand_rows(m, reps):
    g, n = m.shape
    return jnp.broadcast_to(m[:, None, :], (g, reps, n)).reshape(g * reps, n)


def _tri(c, kind):
    r = lax.broadcasted_iota(jnp.int32, (c, c), 0)
    s = lax.broadcasted_iota(jnp.int32, (c, c), 1)
    if kind == "lower":
        return s <= r
    if kind == "strict":
        return s < r
    return s >= r


def _mod_kernel(c_ref, w_ref, b_ref, o_ref):
    s = _silu(c_ref[...])
    o_ref[0] = _dot(s, w_ref[0]) + b_ref[0]


def _modulation(c_all, ada_w, ada_b):
    nb = c_all.shape[0]
    tn = 1024
    return pl.pallas_call(
        _mod_kernel,
        grid=(DEPTH, 6 * D_MODEL // tn),
        in_specs=[pl.BlockSpec((nb, D_MODEL), lambda l, j: (0, 0)),
                  pl.BlockSpec((1, D_MODEL, tn), lambda l, j: (l, 0, j)),
                  pl.BlockSpec((1, 1, tn), lambda l, j: (l, 0, j))],
        out_specs=pl.BlockSpec((1, nb, tn), lambda l, j: (l, 0, j)),
        out_shape=jax.ShapeDtypeStruct((DEPTH, nb, 6 * D_MODEL), F32),
        compiler_params=pltpu.CompilerParams(dimension_semantics=("arbitrary", "arbitrary"),
                                             vmem_limit_bytes=VMEM_LIMIT),
        name="adaln_mod",
    )(c_all, ada_w, ada_b.reshape(DEPTH, 1, 6 * D_MODEL))


def _s5_prep_kernel(are_ref, aim_ref, ldt_ref, btr_ref, bti_ref, cr_ref, ci_ref, arow_ref, e_ref, gm_ref,
                    b_out, c_out, l1_out, pp_out, ps_out):
    a_re, a_im = are_ref[0], aim_ref[0]
    dt = jnp.exp(ldt_ref[0])
    mag = jnp.exp(dt * a_re)
    ab_re, ab_im = mag * jnp.cos(dt * a_im), mag * jnp.sin(dt * a_im)
    den = a_re * a_re + a_im * a_im
    f_re = ((ab_re - 1.0) * a_re + ab_im * a_im) / den
    f_im = (ab_im * a_re - (ab_re - 1.0) * a_im) / den
    bb_re = f_re * btr_ref[0] - f_im * bti_ref[0]
    bb_im = f_re * bti_ref[0] + f_im * btr_ref[0]
    e01, gm = e_ref[...], gm_ref[...]
    b_out[0, 0] = (_dot_x_exact01(bb_re, e01) * gm).astype(BF16)
    b_out[0, 1] = (_dot_x_exact01(bb_im, e01) * gm).astype(BF16)
    c_out[0, 0] = (_dot_x_exact01(cr_ref[0], e01) * gm).astype(BF16)
    c_out[0, 1] = (_dot_x_exact01(ci_ref[0], e01) * gm).astype(BF16)
    ar, ai, dtr = arow_ref[0, 0:1, :], arow_ref[0, 1:2, :], jnp.exp(arow_ref[0, 2:3, :])
    row = lax.broadcasted_iota(jnp.int32, (SUBLANES, 1), 0)

    def power(k):
        m = jnp.exp(k * dtr * ar)
        return m * jnp.cos(k * dtr * ai), m * jnp.sin(k * dtr * ai)

    for i, d in enumerate((1, 2, 4)):
        pr, pi = power(jnp.full((SUBLANES, 1), float(d), F32))
        keep = row >= d
        l1_out[0, i] = jnp.where(keep, pr, 0.0)
        l1_out[0, 3 + i] = jnp.where(keep, pi, 0.0)
    pr, pi = power((row + 1).astype(F32))
    pp_out[0, 0], pp_out[0, 1] = pr, pi
    pr, pi = power(jnp.maximum(row - (SLOT_OFF - 1), 0).astype(F32))
    ps_out[0, 0] = jnp.where(row >= SLOT_OFF, pr, 0.0)
    ps_out[0, 1] = jnp.where(row >= SLOT_OFF, pi, 0.0)


def _s5_prepare(s5_a_re, s5_a_im, s5_log_dt, s5_b_re, s5_b_im, s5_c_re, s5_c_im):
    rows = S5_GROUPS * S5_GROUP_CH
    rep = lambda a: jnp.repeat(a, S5_GROUP_CH, axis=1)
    ldt = jnp.broadcast_to(s5_log_dt[:, :, None], (DEPTH, S5_GROUPS, S5_STATE))
    bt = lambda b: jnp.swapaxes(b, 2, 3).reshape(DEPTH, rows, S5_STATE)
    arow = jnp.stack([s5_a_re.reshape(DEPTH, S5_WIDTH), s5_a_im.reshape(DEPTH, S5_WIDTH),
                      ldt.reshape(DEPTH, S5_WIDTH)], axis=1)
    arow = jnp.pad(arow, ((0, 0), (0, SUBLANES - 3), (0, 0)))
    e01 = np.zeros((S5_STATE, S5_WIDTH), np.float32)
    e01[np.arange(S5_WIDTH) % S5_STATE, np.arange(S5_WIDTH)] = 1.0
    gm = (np.arange(rows)[:, None] // S5_GROUP_CH == np.arange(S5_WIDTH)[None, :] // S5_STATE).astype(np.float32)
    p3 = lambda: pl.BlockSpec((1, rows, S5_STATE), lambda l: (l, 0, 0))
    tab = lambda n: pl.BlockSpec((1, n, SUBLANES, S5_WIDTH), lambda l: (l, 0, 0, 0))
    return pl.pallas_call(
        _s5_prep_kernel,
        grid=(DEPTH,),
        in_specs=[p3(), p3(), p3(), p3(), p3(), p3(), p3(),
                  pl.BlockSpec((1, SUBLANES, S5_WIDTH), lambda l: (l, 0, 0)),
                  pl.BlockSpec((S5_STATE, S5_WIDTH), lambda l: (0, 0)),
                  pl.BlockSpec((rows, S5_WIDTH), lambda l: (0, 0))],
        out_specs=[pl.BlockSpec((1, 2, rows, S5_WIDTH), lambda l: (l, 0, 0, 0)),
                   pl.BlockSpec((1, 2, rows, S5_WIDTH), lambda l: (l, 0, 0, 0)),
                   tab(6), tab(2), tab(2)],
        out_shape=[jax.ShapeDtypeStruct((DEPTH, 2, rows, S5_WIDTH), BF16),
                   jax.ShapeDtypeStruct((DEPTH, 2, rows, S5_WIDTH), BF16),
                   jax.ShapeDtypeStruct((DEPTH, 6, SUBLANES, S5_WIDTH), F32),
                   jax.ShapeDtypeStruct((DEPTH, 2, SUBLANES, S5_WIDTH), F32),
                   jax.ShapeDtypeStruct((DEPTH, 2, SUBLANES, S5_WIDTH), F32)],
        compiler_params=pltpu.CompilerParams(dimension_semantics=("arbitrary",), vmem_limit_bytes=VMEM_LIMIT),
        name="s5_prepare",
    )(rep(s5_a_re), rep(s5_a_im), rep(ldt), bt(s5_b_re), bt(s5_b_im),
      s5_c_re.reshape(DEPTH, rows, S5_STATE), s5_c_im.reshape(DEPTH, rows, S5_STATE),
      arow, jnp.asarray(e01, BF16), jnp.asarray(gm, F32))


def _inproj_kernel(x_ref, sh_ref, sc_ref, g_ref, w_ref, o_ref, *, sample):
    x = x_ref[...]
    hn = _rms(x, g_ref[...])
    if sample:
        sh, sc = _expand_rows(sh_ref[...], SLOT), _expand_rows(sc_ref[...], SLOT)
    else:
        sh, sc = sh_ref[0], sc_ref[0]
    h = (hn * (1.0 + sc) + sh).astype(BF16)
    nc = 640
    for j in range(0, NZ, nc):
        o_ref[:, j:j + nc] = jnp.dot(h, w_ref[:, j:j + nc], preferred_element_type=F32)


def _mod_specs(mod_l, chunk_ids, tm, tiles_per_seq, sample):
    if sample:
        return ([pl.BlockSpec((tm // SLOT, D_MODEL), (lambda i, j=j: (i, j))) for j in chunk_ids],
                [mod_l] * len(chunk_ids))
    m3 = mod_l.reshape(mod_l.shape[0] * 6, 1, D_MODEL)
    return ([pl.BlockSpec((1, 1, D_MODEL), (lambda i, j=j: ((i // tiles_per_seq) * 6 + j, 0, 0)))
             for j in chunk_ids], [m3] * len(chunk_ids))


def _const_spec(shape):
    nd = len(shape)
    return pl.BlockSpec(shape, lambda i, _nd=nd: (0,) * _nd, pipeline_mode=pl.Buffered(1))


def _input_projection(x2d, mod_l, gain, w_bf16, *, rows_per_seq, sample):
    t = x2d.shape[0]
    tm = min(TILE_TOKENS, t)
    tiles_per_seq = max(rows_per_seq // tm, 1)
    mspecs, mops = _mod_specs(mod_l, (0, 1), tm, tiles_per_seq, sample)
    return pl.pallas_call(
        functools.partial(_inproj_kernel, sample=sample),
        grid=(t // tm,),
        in_specs=[pl.BlockSpec((tm, D_MODEL), lambda i: (i, 0))] + mspecs
                 + [_const_spec((1, D_MODEL)), _const_spec((D_MODEL, NZ))],
        out_specs=pl.BlockSpec((tm, NZ), lambda i: (i, 0)),
        out_shape=jax.ShapeDtypeStruct((t, NZ), F32),
        compiler_params=pltpu.CompilerParams(dimension_semantics=("arbitrary",), vmem_limit_bytes=VMEM_LIMIT),
        name="in_projection",
    )(x2d, *mops, gain.reshape(1, D_MODEL), w_bf16)


def _causal_conv(ext_ref, xin, w_ref, width, tl):
    ext_ref[pl.ds(SUBLANES, tl), :] = xin
    acc = None
    for k in range(width):
        term = w_ref[k:k + 1, :] * ext_ref[pl.ds(SUBLANES - (width - 1) + k, tl), :]
        acc = term if acc is None else acc + term
    return acc


def _head_meansq(x, e01, width):
    return _dot_x_exact01(x * x, e01) * (1.0 / width)


def _mixers_kernel(*refs, tl, sample, c_scan, c_gla):
    (zin, psm, ssd_cw, ssd_cb, ssd_nrm, s5_b, s5_ct, s5_l1, s5_p, s5_d, glu_w, s5_nrm,
     gdn_cw, gdn_nrm, gla_w2, gla_bg, gla_nrm, e01) = refs[:18]
    n_in = 18
    if sample:
        st_ssd, hist_ssd, st_s5r, st_s5i, st_gdn, hist_gdn, st_gla = refs[18:25]
        n_in = 25
    outs = refs[n_in:n_in + 8]
    ymix, o_ssd, o_ssd_cv, o_s5r, o_s5i, o_gdn, o_gdn_cv, o_gla = outs
    (ext_ssd, ext_gdn, s_xbc, s_dt, s_la, s_beta, s_qkv, s_lg, s_y, s_hre, s_him) = refs[n_in + 8:]

    rows = lax.broadcasted_iota(jnp.int32, (tl, 1), 0)
    valid = (rows & (SLOT - 1)) >= SLOT_OFF if sample else None
    first = None if sample else (pl.program_id(1) == 0)

    def masked(x):
        return jnp.where(valid, x, 0.0) if sample else x

    if sample:
        ext_ssd[0:SUBLANES, :] = jnp.zeros((SUBLANES, SSD_CONV_DIM), F32)
        ext_gdn[0:SUBLANES, :] = jnp.zeros((SUBLANES, GDN_CONV_DIM), F32)
        st_in = (st_ssd, st_gdn, st_gla)
    else:
        @pl.when(first)
        def _():
            ext_ssd[0:SUBLANES, :] = jnp.zeros((SUBLANES, SSD_CONV_DIM), F32)
            ext_gdn[0:SUBLANES, :] = jnp.zeros((SUBLANES, GDN_CONV_DIM), F32)
            o_ssd[...] = jnp.zeros(o_ssd.shape, F32)
            o_gdn[...] = jnp.zeros(o_gdn.shape, F32)
            o_gla[...] = jnp.zeros(o_gla.shape, F32)
            o_s5r[...] = jnp.zeros(o_s5r.shape, F32)
            o_s5i[...] = jnp.zeros(o_s5i.shape, F32)
        st_in = (o_ssd, o_gdn, o_gla)

    def conv_in(ext_ref, col, hist_ref, w_ref, o_cv):
        xin = zin[:, col:col + 768]
        if sample:
            xin = jnp.where(valid, xin, hist_ref[...])
        y = _causal_conv(ext_ref, xin, w_ref, CONV_W, tl)
        if sample:
            o_cv[...] = xin
        else:
            tail = ext_ref[pl.ds(tl + SUBLANES - (CONV_W - 1), CONV_W - 1), :]
            o_cv[0] = tail
            ext_ref[pl.ds(SUBLANES - (CONV_W - 1), CONV_W - 1), :] = tail
        return y

    s_xbc[...] = _silu(conv_in(ext_ssd, Z_XBC, hist_ssd if sample else None, ssd_cw, o_ssd_cv) + ssd_cb[...])

    sm = zin[:, Z_SM:Z_SM + 128]
    sp = _softplus(sm + psm[0:1, :])
    s_dt[...] = masked(sp)
    s_la[...] = masked(-jnp.exp(psm[1:2, :]) * sp)
    s_beta[...] = masked(_sigmoid(sm))
    s_lg[...] = masked(-_softplus(-(_dot(sm, gla_w2[...]) + gla_bg[...])) * (1.0 / GLA_TAU))

    qkv = _silu(conv_in(ext_gdn, Z_QKV, hist_gdn if sample else None, gdn_cw, o_gdn_cv))
    qk = qkv[:, 0:512]
    inv = lax.rsqrt(_dot_x_exact01(qk * qk, e01[...]) + EPS)
    s_qkv[:, 0:256] = qk[:, 0:256] * inv[:, 0:256] * (GDN_HEADDIM ** -0.5)
    s_qkv[:, 256:512] = qk[:, 256:512] * inv[:, 256:512]
    s_qkv[:, 512:768] = qkv[:, 512:768]

    u = zin[:, Z_U:Z_U + 256]
    n8 = tl // SUBLANES
    x_re = masked(_dot(u, s5_b[0])).reshape(n8, SUBLANES, S5_WIDTH)
    x_im = masked(_dot(u, s5_b[1])).reshape(n8, SUBLANES, S5_WIDTH)
    for i, d in enumerate((1, 2, 4)):
        sr, si = pltpu.roll(x_re, d, 1), pltpu.roll(x_im, d, 1)
        ar, ai = s5_l1[i][None], s5_l1[3 + i][None]
        x_re, x_im = x_re + ar * sr - ai * si, x_im + ar * si + ai * sr
    p_re, p_im = s5_p[0], s5_p[1]
    if sample:
        c_re, c_im = st_s5r[...][:, None, :], st_s5i[...][:, None, :]
        h_re = (x_re + p_re[None] * c_re - p_im[None] * c_im).reshape(tl, S5_WIDTH)
        h_im = (x_im + p_re[None] * c_im + p_im[None] * c_re).reshape(tl, S5_WIDTH)
        o_s5r[...] = h_re
        o_s5i[...] = h_im
    else:
        s_hre[...] = x_re.reshape(tl, S5_WIDTH)
        s_him[...] = x_im.reshape(tl, S5_WIDTH)

        def s5_body(j, carry):
            c_re, c_im = carry
            rs = pl.ds(pl.multiple_of(j * SUBLANES, SUBLANES), SUBLANES)
            hr = s_hre[rs, :] + p_re * c_re - p_im * c_im
            hi = s_him[rs, :] + p_re * c_im + p_im * c_re
            s_hre[rs, :] = hr
            s_him[rs, :] = hi
            return hr[SUBLANES - 1:SUBLANES, :], hi[SUBLANES - 1:SUBLANES, :]

        c_re, c_im = lax.fori_loop(0, n8, s5_body, (o_s5r[0], o_s5i[0]))
        o_s5r[0] = c_re
        o_s5i[0] = c_im
        h_re, h_im = s_hre[...], s_him[...]
    y5 = _dot_nt(h_re, s5_ct[0]) - _dot_nt(h_im, s5_ct[1]) + s5_d[...] * u
    yy = _dot(y5, glu_w[...])
    y5 = _rms(yy[:, 0:256] * _sigmoid(yy[:, 256:512]), s5_nrm[...])
    ymix[:, 256:512] = masked(y5)

    c = c_scan
    lower, strict, upper = _tri(c, "lower"), _tri(c, "strict"), _tri(c, "upper")
    lower01, upper01 = lower.astype(BF16), upper.astype(BF16)
    ssd_in, gdn_in, gla_in = st_in

    def scan_body(i, carry):
        sidx = i if sample else 0
        rs = pl.ds(pl.multiple_of(i * c, c), c)
        la = s_la[rs, :]
        cum = _dot_exact01(lower01, la)
        cum_t = _dot_exact01_tn(la, upper01)
        cl = cum[c - 1:c, :]
        tail, ecum, ecl = jnp.exp(cl - cum), jnp.exp(cum), jnp.exp(cl)
        dt, beta = s_dt[rs, :], s_beta[rs, :]
        xbc = s_xbc[rs, :]
        for g in range(2):
            bg = xbc[:, 256 + g * SSD_STATE:256 + (g + 1) * SSD_STATE]
            cg = xbc[:, 512 + g * SSD_STATE:512 + (g + 1) * SSD_STATE]
            cb = _dot_nt(cg, bg)
            for h in (2 * g, 2 * g + 1):
                ln = SM_DT + h
                dec = jnp.where(lower, jnp.exp(cum[:, ln:ln + 1] - cum_t[ln:ln + 1, :]), 0.0)
                xh = xbc[:, h * SSD_HEADDIM:(h + 1) * SSD_HEADDIM]
                xq = xh * dt[:, ln:ln + 1]
                h0 = ssd_in[sidx, h]
                y = _dot(cb * dec, xq) + _dot(cg * ecum[:, ln:ln + 1], h0) + psm[2:3, h:h + 1] * xh
                o_ssd[sidx, h] = h0 * ecl[:, ln:ln + 1] + _dot_tn(bg * tail[:, ln:ln + 1], xq)
                s_y[rs, h * SSD_HEADDIM:(h + 1) * SSD_HEADDIM] = y
        qkvc = s_qkv[rs, :]
        for h in range(GDN_HEADS):
            ln = SM_GA + h
            q = qkvc[:, h * GDN_HEADDIM:(h + 1) * GDN_HEADDIM]
            k = qkvc[:, 256 + h * GDN_HEADDIM:256 + (h + 1) * GDN_HEADDIM]
            v = qkvc[:, 512 + h * GDN_HEADDIM:512 + (h + 1) * GDN_HEADDIM]
            dec = jnp.where(lower, jnp.exp(cum[:, ln:ln + 1] - cum_t[ln:ln + 1, :]), 0.0)
            b = beta[:, SM_GB + h:SM_GB + h + 1]
            a_mat = jnp.where(strict, dec * _dot_nt(k, k), 0.0) * b
            qk_d = _dot_nt(q, k) * dec
            rhs = jnp.concatenate([v * b, k * (b * ecum[:, ln:ln + 1])], axis=1)
            x = rhs - _dot_f32(a_mat, rhs)
            p, pw = a_mat, 2
            while pw < c:
                p = _dot_f32(p, p)
                x = x + _dot_f32(p, x)
                pw *= 2
            s0 = gdn_in[sidx, h]
            delta = x[:, 0:GDN_HEADDIM] - _dot(x[:, GDN_HEADDIM:2 * GDN_HEADDIM], s0)
            o = _dot(q * ecum[:, ln:ln + 1], s0) + _dot(qk_d, delta)
            o_gdn[sidx, h] = s0 * ecl[:, ln:ln + 1] + _dot_tn(k * tail[:, ln:ln + 1], delta)
            s_y[rs, 512 + h * GDN_HEADDIM:512 + (h + 1) * GDN_HEADDIM] = o
        return carry

    lax.fori_loop(0, tl // c, scan_body, 0)

    cg_ = c_gla
    lower_g = _tri(cg_, "lower")
    lower_g01 = lower_g.astype(BF16)
    valid_c = (lax.broadcasted_iota(jnp.int32, (cg_, 1), 0) >= SLOT_OFF) if sample else None

    def gla_body(i, carry):
        sidx = i if sample else 0
        rs = pl.ds(pl.multiple_of(i * cg_, cg_), cg_)
        cum = _dot_exact01(lower_g01, s_lg[rs, :])
        ref = cum[cg_ // 2 - 1:cg_ // 2, :]
        cl = cum[cg_ - 1:cg_, :]
        q = zin[rs, Z_Q:Z_Q + 128] * (GLA_DK ** -0.5)
        k = zin[rs, Z_K:Z_K + 128]
        if sample:
            k = jnp.where(valid_c, k, 0.0)
        v = zin[rs, Z_V:Z_V + 256]
        qe, ke = q * jnp.exp(cum - ref), k * jnp.exp(ref - cum)
        qd, kt, ecl = q * jnp.exp(cum), k * jnp.exp(cl - cum), jnp.exp(cl)
        for h in range(GLA_HEADS):
            ks = slice(h * GLA_DK, (h + 1) * GLA_DK)
            vh = v[:, h * GLA_DV:(h + 1) * GLA_DV]
            sc = jnp.where(lower_g, _dot_nt(qe[:, ks], ke[:, ks]), 0.0)
            st = gla_in[sidx, h]
            o = _dot(sc, vh) + _dot_nt(qd[:, ks], st)
            o_gla[sidx, h] = st * ecl[:, ks] + _dot_tn(vh, kt[:, ks])
            s_y[rs, 768 + h * GLA_DV:768 + (h + 1) * GLA_DV] = o
        return carry

    lax.fori_loop(0, tl // cg_, gla_body, 0)

    e256 = e01[0:256, 0:256]
    y = s_y[:, 0:256] * _silu(zin[:, Z_Z:Z_Z + 256])
    ymix[:, 0:256] = masked(_rms(y, ssd_nrm[...]))
    o = s_y[:, 512:768]
    o = o * lax.rsqrt(_head_meansq(o, e256, GDN_HEADDIM) + EPS) * gdn_nrm[...]
    ymix[:, 512:768] = masked(o * _silu(zin[:, Z_GZ:Z_GZ + 256]))
    o = s_y[:, 768:1024]
    o = o * lax.rsqrt(_head_meansq(o, e256, GLA_DV) + EPS) * gla_nrm[...]
    ymix[:, 768:1024] = masked(o * _silu(zin[:, Z_GG:Z_GG + 256]))


def _mixers(zin, lp, *, batch, rows_per_seq, sample, states=None):
    t = zin.shape[0]
    tl = min(SAMPLE_MIX_TILE if sample else TILE_TOKENS, t)
    if sample:
        slots = tl // SLOT
        grid = (t // tl,)
        tok = lambda w: pl.BlockSpec((tl, w), lambda i: (i, 0))
        st4 = lambda a, b_, c_: pl.BlockSpec((slots, a, b_, c_), lambda i: (i, 0, 0, 0))
        cmap = lambda nd: (lambda i: (0,) * nd)
        c_scan = c_gla = SLOT
    else:
        tiles = rows_per_seq // tl
        grid = (batch, tiles)
        tok = lambda w: pl.BlockSpec((tl, w), lambda b, l: (b * tiles + l, 0))
        st4 = lambda a, b_, c_: pl.BlockSpec((1, a, b_, c_), lambda b, l: (b, 0, 0, 0))
        cmap = lambda nd: (lambda b, l: (0,) * nd)
        c_scan, c_gla = SCAN_CHUNK, GLA_CHUNK
    const = lambda a: pl.BlockSpec(a.shape, cmap(a.ndim), pipeline_mode=pl.Buffered(1))
    consts = [lp["psmall"], lp["ssd_cw"], lp["ssd_cb"], lp["ssd_nrm"], lp["s5_b"], lp["s5_ct"], lp["s5_l1"],
              lp["s5_ps"] if sample else lp["s5_pp"], lp["s5_d"], lp["glu_w"], lp["s5_nrm"],
              lp["gdn_cw"], lp["gdn_nrm"], lp["gla_w2"], lp["gla_bg"], lp["gla_nrm"], lp["e01"]]
    in_specs = [tok(NZ)] + [const(a) for a in consts]
    operands = [zin] + consts
    nseq = t // SLOT if sample else batch
    if sample:
        in_specs += [st4(SSD_HEADS, SSD_STATE, SSD_HEADDIM), tok(SSD_CONV_DIM),
                     pl.BlockSpec((slots, S5_WIDTH), lambda i: (i, 0)),
                     pl.BlockSpec((slots, S5_WIDTH), lambda i: (i, 0)),
                     st4(GDN_HEADS, GDN_HEADDIM, GDN_HEADDIM), tok(GDN_CONV_DIM),
                     st4(GLA_HEADS, GLA_DV, GLA_DK)]
        operands += list(states)
        cv_spec = tok(768)
        cv_shape = jax.ShapeDtypeStruct((t, 768), F32)
        s5_spec = tok(S5_WIDTH)
        s5_shape = jax.ShapeDtypeStruct((t, S5_WIDTH), F32)
    else:
        cv_spec = pl.BlockSpec((1, CONV_W - 1, 768), lambda b, l: (b, 0, 0))
        cv_shape = jax.ShapeDtypeStruct((batch, CONV_W - 1, 768), F32)
        s5_spec = pl.BlockSpec((1, 1, S5_WIDTH), lambda b, l: (b, 0, 0))
        s5_shape = jax.ShapeDtypeStruct((batch, 1, S5_WIDTH), F32)
    out_specs = [tok(D_MODEL), st4(SSD_HEADS, SSD_STATE, SSD_HEADDIM), cv_spec, s5_spec, s5_spec,
                 st4(GDN_HEADS, GDN_HEADDIM, GDN_HEADDIM), cv_spec, st4(GLA_HEADS, GLA_DV, GLA_DK)]
    out_shape = [jax.ShapeDtypeStruct((t, D_MODEL), F32),
                 jax.ShapeDtypeStruct((nseq, SSD_HEADS, SSD_STATE, SSD_HEADDIM), F32), cv_shape, s5_shape, s5_shape,
                 jax.ShapeDtypeStruct((nseq, GDN_HEADS, GDN_HEADDIM, GDN_HEADDIM), F32), cv_shape,
                 jax.ShapeDtypeStruct((nseq, GLA_HEADS, GLA_DV, GLA_DK), F32)]
    scratch = [pltpu.VMEM((tl + SUBLANES, 768), F32), pltpu.VMEM((tl + SUBLANES, 768), F32),
               pltpu.VMEM((tl, 768), F32), pltpu.VMEM((tl, 128), F32), pltpu.VMEM((tl, 128), F32),
               pltpu.VMEM((tl, 128), F32), pltpu.VMEM((tl, 768), F32), pltpu.VMEM((tl, 128), F32),
               pltpu.VMEM((tl, D_MODEL), F32), pltpu.VMEM((tl, S5_WIDTH), F32), pltpu.VMEM((tl, S5_WIDTH), F32)]
    return pl.pallas_call(
        functools.partial(_mixers_kernel, tl=tl, sample=sample, c_scan=c_scan, c_gla=c_gla),
        grid=grid, in_specs=in_specs, out_specs=out_specs, out_shape=out_shape, scratch_shapes=scratch,
        compiler_params=pltpu.CompilerParams(dimension_semantics=("arbitrary",) * len(grid),
                                             vmem_limit_bytes=VMEM_LIMIT),
        name="mixers_sample" if sample else "mixers_prompt",
    )(*operands)


def _outffn_kernel(*refs, tm, sample, final):
    x_ref, y_ref, g1_ref, sh_ref, sc_ref, g2_ref, nf_ref, wout, up, cw, cb, down = refs[:12]
    k = 12
    hist = fin = None
    if sample:
        hist = refs[k]
        k += 1
    if final:
        fin = refs[k]
        k += 1
    x2_ref, cv_ref, ext = refs[k], refs[k + 1], refs[k + 2]

    if sample:
        ex = lambda r: _expand_rows(r[...], SLOT)
        g1, sh, sc, g2 = ex(g1_ref), ex(sh_ref), ex(sc_ref), ex(g2_ref)
        valid = (lax.broadcasted_iota(jnp.int32, (tm, 1), 0) & (SLOT - 1)) >= SLOT_OFF
        ext[0:SUBLANES, :] = jnp.zeros((SUBLANES, 2 * D_FF), F32)
    else:
        g1, sh, sc, g2 = g1_ref[0], sh_ref[0], sc_ref[0], g2_ref[0]

        @pl.when(pl.program_id(1) == 0)
        def _():
            ext[0:SUBLANES, :] = jnp.zeros((SUBLANES, 2 * D_FF), F32)

    x1 = x_ref[...] + g1 * _dot(y_ref[...], wout[...])
    h = (_rms(x1, nf_ref[...]) * (1.0 + sc) + sh).astype(BF16)
    for j in range(0, 2 * D_FF, FF_CHUNK):
        uj = jnp.dot(h, up[:, j:j + FF_CHUNK], preferred_element_type=F32)
        if sample:
            uj = jnp.where(valid, uj, hist[:, j:j + FF_CHUNK])
        ext[pl.ds(SUBLANES, tm), j:j + FF_CHUNK] = uj

    def conv(col):
        acc = cb[:, col:col + FF_CHUNK]
        for kk in range(FFN_CONV_W):
            acc = acc + cw[kk:kk + 1, col:col + FF_CHUNK] * ext[pl.ds(SUBLANES - (FFN_CONV_W - 1) + kk, tm),
                                                                 col:col + FF_CHUNK]
        return acc

    f = jnp.zeros((tm, D_MODEL), F32)
    for j in range(0, D_FF, FF_CHUNK):
        act = _silu(conv(j)) * conv(D_FF + j)
        f = f + _dot(act, down[j:j + FF_CHUNK, :])
    x2 = x1 + g2 * f
    if final:
        x2 = _rms(x2, fin[...])
    if sample:
        x2_ref[...] = jnp.where(valid, x2, 0.0)
        cv_ref[...] = ext[pl.ds(SUBLANES, tm), :]
    else:
        x2_ref[...] = x2
        tail = ext[pl.ds(tm + SUBLANES - (FFN_CONV_W - 1), FFN_CONV_W - 1), :]
        cv_ref[0] = tail
        ext[pl.ds(SUBLANES - (FFN_CONV_W - 1), FFN_CONV_W - 1), :] = tail


def _out_ffn(x2d, ymix, mod_l, lp, *, batch, rows_per_seq, sample, hist=None, final_gain=None):
    t = x2d.shape[0]
    tm = min(SAMPLE_FFN_TILE if sample else TILE_TOKENS, t)
    final = final_gain is not None
    if sample:
        grid = (t // tm,)
        tok = lambda w: pl.BlockSpec((tm, w), lambda i: (i, 0))
        cmap = lambda nd: (lambda i: (0,) * nd)
        mspecs = [pl.BlockSpec((tm // SLOT, D_MODEL), (lambda i, j=j: (i, j))) for j in (2, 3, 4, 5)]
        mops = [mod_l] * 4
        cv_spec, cv_shape = tok(2 * D_FF), jax.ShapeDtypeStruct((t, 2 * D_FF), F32)
    else:
        tiles = rows_per_seq // tm
        grid = (batch, tiles)
        tok = lambda w: pl.BlockSpec((tm, w), lambda b, l: (b * tiles + l, 0))
        cmap = lambda nd: (lambda b, l: (0,) * nd)
        m3 = mod_l.reshape(batch * 6, 1, D_MODEL)
        mspecs = [pl.BlockSpec((1, 1, D_MODEL), (lambda b, l, j=j: (b * 6 + j, 0, 0))) for j in (2, 3, 4, 5)]
        mops = [m3] * 4
        cv_spec = pl.BlockSpec((1, FFN_CONV_W - 1, 2 * D_FF), lambda b, l: (b, 0, 0))
        cv_shape = jax.ShapeDtypeStruct((batch, FFN_CONV_W - 1, 2 * D_FF), F32)
    const = lambda a: pl.BlockSpec(a.shape, cmap(a.ndim), pipeline_mode=pl.Buffered(1))
    consts = [lp["norm_ffn"], lp["w_out"], lp["ffn_up"], lp["ffn_cw"], lp["ffn_cb"], lp["ffn_down"]]
    in_specs = [tok(D_MODEL), tok(D_MODEL)] + mspecs + [const(a) for a in consts]
    operands = [x2d, ymix] + mops + consts
    if sample:
        in_specs.append(tok(2 * D_FF))
        operands.append(hist)
    if final:
        in_specs.append(const(final_gain))
        operands.append(final_gain)
    return pl.pallas_call(
        functools.partial(_outffn_kernel, tm=tm, sample=sample, final=final),
        grid=grid, in_specs=in_specs,
        out_specs=[tok(D_MODEL), cv_spec],
        out_shape=[jax.ShapeDtypeStruct((t, D_MODEL), F32), cv_shape],
        scratch_shapes=[pltpu.VMEM((tm + SUBLANES, 2 * D_FF), F32)],
        compiler_params=pltpu.CompilerParams(dimension_semantics=("arbitrary",) * len(grid),
                                             vmem_limit_bytes=VMEM_LIMIT),
        name="out_ffn_sample" if sample else "out_ffn_prompt",
    )(*operands)


def _reorder_w_in(w_in):
    o = np.cumsum([0, 256, 768, 4, 256, 768, 256, 4, 4, 128, 128, 256, 256, 16])
    seg = lambda i: w_in[:, :, o[i]:o[i + 1]]
    small = [seg(2), seg(6), seg(7), seg(12)]
    pad = jnp.zeros(w_in.shape[:2] + (128 - 28,), w_in.dtype)
    cols = [seg(1), seg(4), seg(0), seg(3), seg(5), seg(10), seg(11), seg(8), seg(9)] + small + [pad]
    return jnp.concatenate(cols, axis=-1).astype(BF16)


def _pad_lanes(a, lane0, width=128):
    return jnp.pad(a, ((0, 0), (lane0, width - lane0 - a.shape[1])))[:, None, :]


def _slot_rows(a, first_row):
    b, r, c = a.shape
    return jnp.pad(a, ((0, 0), (first_row, SLOT - first_row - r), (0, 0))).reshape(b * SLOT, c)


def kernel(x_prompt, x_sample, c_prompt, c_sample, state_ssd, state_ssd_conv, state_s5_re, state_s5_im,
           state_gdn, state_gdn_conv, state_gla, state_ffn_conv, ada_w, ada_b, norm_mix, norm_ffn, w_in, w_out,
           ssd_conv_w, ssd_conv_b, ssd_dt_bias, ssd_a_log, ssd_d, ssd_norm, s5_a_re, s5_a_im, s5_log_dt,
           s5_b_re, s5_b_im, s5_c_re, s5_c_im, s5_d, s5_glu_w, s5_norm, gdn_conv_w, gdn_a_log, gdn_dt_bias,
           gdn_norm, gla_wg2, gla_bg, gla_norm, ffn_up, ffn_conv_w, ffn_conv_b, ffn_down, final_norm):
    nb, seq = x_prompt.shape[0], x_prompt.shape[1]
    ns, dseq = x_sample.shape[0], x_sample.shape[1]
    assert dseq == SLOT - SLOT_OFF

    mod = _modulation(jnp.concatenate([c_prompt, c_sample], axis=0).astype(F32), ada_w, ada_b)
    s5_b, s5_ct, s5_l1, s5_pp, s5_ps = _s5_prepare(s5_a_re, s5_a_im, s5_log_dt, s5_b_re, s5_b_im, s5_c_re, s5_c_im)

    w_in_r = _reorder_w_in(w_in)
    w_out_b, up_b, down_b, glu_b = w_out.astype(BF16), ffn_up.astype(BF16), ffn_down.astype(BF16), s5_glu_w.astype(BF16)
    psmall = jnp.concatenate([
        _pad_lanes(ssd_dt_bias, SM_DT) + _pad_lanes(gdn_dt_bias, SM_GA),
        _pad_lanes(ssd_a_log, SM_DT) + _pad_lanes(gdn_a_log, SM_GA),
        _pad_lanes(ssd_d, 0), jnp.zeros((DEPTH, SUBLANES - 3, 128), F32)], axis=1)
    gla_w2 = jnp.pad(gla_wg2, ((0, 0), (SM_LR, 128 - SM_LR - GLA_GATE_RANK), (0, 0))).astype(BF16)
    e01 = jnp.asarray(np.arange(512)[:, None] // 64 == np.arange(512)[None, :] // 64, BF16)
    row = lambda a: a[:, None, :]

    layers = []
    for l in range(DEPTH):
        layers.append(dict(
            psmall=psmall[l], ssd_cw=ssd_conv_w[l], ssd_cb=row(ssd_conv_b)[l], ssd_nrm=row(ssd_norm)[l],
            s5_b=s5_b[l], s5_ct=s5_ct[l], s5_l1=s5_l1[l], s5_pp=s5_pp[l], s5_ps=s5_ps[l], s5_d=row(s5_d)[l],
            glu_w=glu_b[l], s5_nrm=row(s5_norm)[l], gdn_cw=gdn_conv_w[l],
            gdn_nrm=jnp.tile(gdn_norm[l], GDN_HEADS)[None, :], gla_w2=gla_w2[l], gla_bg=row(gla_bg)[l],
            gla_nrm=jnp.tile(gla_norm[l], GLA_HEADS)[None, :], e01=e01,
            norm_mix=norm_mix[l], norm_ffn=row(norm_ffn)[l], w_in=w_in_r[l], w_out=w_out_b[l],
            ffn_up=up_b[l], ffn_cw=ffn_conv_w[l], ffn_cb=row(ffn_conv_b)[l], ffn_down=down_b[l]))
    fin = final_norm.reshape(1, D_MODEL)

    xp = x_prompt.astype(F32).reshape(nb * seq, D_MODEL)
    p_new = []
    for l, lp in enumerate(layers):
        mod_l = mod[l, :nb]
        zin = _input_projection(xp, mod_l, lp["norm_mix"], lp["w_in"], rows_per_seq=seq, sample=False)
        ymix, h_ssd, cv_ssd, h5r, h5i, s_gdn, cv_gdn, s_gla = _mixers(zin, lp, batch=nb, rows_per_seq=seq, sample=False)
        xp, cv_ffn = _out_ffn(xp, ymix, mod_l, lp, batch=nb, rows_per_seq=seq, sample=False,
                              final_gain=fin if l == DEPTH - 1 else None)
        p_new.append((h_ssd, cv_ssd, h5r.reshape(nb, S5_GROUPS, S5_STATE), h5i.reshape(nb, S5_GROUPS, S5_STATE),
                      s_gdn, cv_gdn, jnp.swapaxes(s_gla, 2, 3), cv_ffn))

    xs = jnp.pad(x_sample.astype(F32), ((0, 0), (SLOT_OFF, 0), (0, 0))).reshape(ns * SLOT, D_MODEL)
    s_new = []
    for l, lp in enumerate(layers):
        mod_l = mod[l, nb:]
        states = (state_ssd[l].astype(F32), _slot_rows(state_ssd_conv[l].astype(F32), SLOT_OFF - (CONV_W - 1)),
                  state_s5_re[l].astype(F32).reshape(ns, S5_WIDTH), state_s5_im[l].astype(F32).reshape(ns, S5_WIDTH),
                  state_gdn[l].astype(F32), _slot_rows(state_gdn_conv[l].astype(F32), SLOT_OFF - (CONV_W - 1)),
                  jnp.swapaxes(state_gla[l].astype(F32), 2, 3))
        zin = _input_projection(xs, mod_l, lp["norm_mix"], lp["w_in"], rows_per_seq=SLOT, sample=True)
        ymix, h_ssd, cv_ssd, h5r, h5i, s_gdn, cv_gdn, s_gla = _mixers(zin, lp, batch=ns, rows_per_seq=SLOT,
                                                                      sample=True, states=states)
        hist = _slot_rows(state_ffn_conv[l].astype(F32), SLOT_OFF - (FFN_CONV_W - 1))
        xs, cv_ffn = _out_ffn(xs, ymix, mod_l, lp, batch=ns, rows_per_seq=SLOT, sample=True, hist=hist,
                              final_gain=fin if l == DEPTH - 1 else None)
        last = lambda a, n: a.reshape(ns, SLOT, a.shape[-1])[:, SLOT - n:, :]
        s_new.append((h_ssd, last(cv_ssd, CONV_W - 1),
                      last(h5r, 1).reshape(ns, S5_GROUPS, S5_STATE), last(h5i, 1).reshape(ns, S5_GROUPS, S5_STATE),
                      s_gdn, last(cv_gdn, CONV_W - 1), jnp.swapaxes(s_gla, 2, 3), last(cv_ffn, FFN_CONV_W - 1)))

    y_p = xp.reshape(nb, seq, D_MODEL).astype(x_prompt.dtype)
    y_s = xs.reshape(ns, SLOT, D_MODEL)[:, SLOT_OFF:, :].astype(x_sample.dtype)
    p_st = [jnp.stack(t) for t in zip(*p_new)]
    s_st = [jnp.stack(t) for t in zip(*s_new)]
    return (y_p, y_s, *p_st, *s_st)
```

```python
import functools

import numpy as np
import jax
import jax.numpy as jnp
from jax import lax
from jax.experimental import pallas as pl
from jax.experimental.pallas import tpu as pltpu

F32 = jnp.float32
BF16 = jnp.bfloat16

D_MODEL = 1024
DEPTH = 2
GROUP_WIDTH = 256
CONV_W = 4
SSD_HEADS = 4
SSD_HEADDIM = 64
SSD_STATE = 128
SSD_CONV_DIM = 768
S5_GROUPS = 16
S5_GROUP_CH = 16
S5_STATE = 64
S5_WIDTH = S5_GROUPS * S5_STATE
GDN_HEADS = 4
GDN_HEADDIM = 64
GDN_CONV_DIM = 768
GLA_HEADS = 4
GLA_DK = 32
GLA_DV = 64
GLA_GATE_RANK = 16
GLA_TAU = 16.0
D_FF = 2816
FFN_CONV_W = 3
EPS = 1e-6

NZ = 3200
Z_XBC, Z_QKV, Z_Z, Z_U, Z_GZ, Z_V, Z_GG, Z_Q, Z_K, Z_SM = 0, 768, 1536, 1792, 2048, 2304, 2560, 2816, 2944, 3072
SM_DT, SM_GA, SM_GB, SM_LR = 0, 4, 8, 12

SUBLANES = 8
SLOT = SUBLANES
SLOT_OFF = SLOT - 4
SCAN_CHUNK = 64
GLA_CHUNK = 16
TILE_TOKENS = 256
SAMPLE_MIX_TILE = 128
SAMPLE_FFN_TILE = 128
FF_CHUNK = 256
VMEM_LIMIT = 56 * 1024 * 1024


def _silu(x):
    return x * (1.0 / (1.0 + jnp.exp(-x)))


def _sigmoid(x):
    return 1.0 / (1.0 + jnp.exp(-x))


def _softplus(x):
    return jnp.maximum(x, 0.0) + jnp.log(1.0 + jnp.exp(-jnp.abs(x)))


def _dot(a, b):
    return jnp.dot(a.astype(BF16), b.astype(BF16), preferred_element_type=F32)


def _dot_nt(a, b):
    return lax.dot_general(a.astype(BF16), b.astype(BF16), (((1,), (1,)), ((), ())), preferred_element_type=F32)


def _dot_tn(a, b):
    return lax.dot_general(a.astype(BF16), b.astype(BF16), (((0,), (0,)), ((), ())), preferred_element_type=F32)


def _split_hi_lo(x):
    hi = x.astype(BF16)
    lo = (x - hi.astype(F32)).astype(BF16)
    return hi, lo


def _dot_exact01(m01, x):
    hi, lo = _split_hi_lo(x)
    return jnp.dot(m01, hi, preferred_element_type=F32) + jnp.dot(m01, lo, preferred_element_type=F32)


def _dot_exact01_tn(x, m01):
    hi, lo = _split_hi_lo(x)
    dn = (((0,), (0,)), ((), ()))
    return (lax.dot_general(hi, m01, dn, preferred_element_type=F32)
            + lax.dot_general(lo, m01, dn, preferred_element_type=F32))


def _dot_x_exact01(x, m01):
    hi, lo = _split_hi_lo(x)
    return jnp.dot(hi, m01, preferred_element_type=F32) + jnp.dot(lo, m01, preferred_element_type=F32)


def _unit_lower_inverse(mats, c, blk, merge):
    r = lax.broadcasted_iota(jnp.int32, (c, c), 0)
    s = lax.broadcasted_iota(jnp.int32, (c, c), 1)
    sh = blk.bit_length() - 1
    diag = (r >> sh) == (s >> sh)
    eye = jnp.where(r == s, 1.0, 0.0)
    p = [jnp.where(diag, a, 0.0) for a in mats]
    t = [eye - d for d in p]
    pw = 2
    while pw < blk:
        p = [_dot(d, d) for d in p]
        t = [m + _dot(m, d) for m, d in zip(t, p)]
        pw *= 2
    size = blk
    while merge and size < c:
        sh = size.bit_length() - 1
        pair = ((r >> (sh + 1)) == (s >> (sh + 1))) & ((r >> sh) != (s >> sh))
        lt = [_dot(jnp.where(pair, a, 0.0), m) for a, m in zip(mats, t)]
        t = [m - _dot(m, y) for m, y in zip(t, lt)]
        size *= 2
    return t


def _rms(x, gain):
    return x * lax.rsqrt(jnp.mean(x * x, axis=-1, keepdims=True) + EPS) * gain


def _expand_rows(m, reps):
    g, n = m.shape
    return jnp.broadcast_to(m[:, None, :], (g, reps, n)).reshape(g * reps, n)


def _mod_kernel(c_ref, w_ref, b_ref, o_ref):
    s = _silu(c_ref[...])
    o_ref[0] = _dot(s, w_ref[0]) + b_ref[0]


def _modulation(c_all, ada_w, ada_b):
    nb = c_all.shape[0]
    tn = 1024
    return pl.pallas_call(
        _mod_kernel,
        grid=(DEPTH, 6 * D_MODEL // tn),
        in_specs=[pl.BlockSpec((nb, D_MODEL), lambda l, j: (0, 0)),
                  pl.BlockSpec((1, D_MODEL, tn), lambda l, j: (l, 0, j)),
                  pl.BlockSpec((1, 1, tn), lambda l, j: (l, 0, j))],
        out_specs=pl.BlockSpec((1, nb, tn), lambda l, j: (l, 0, j)),
        out_shape=jax.ShapeDtypeStruct((DEPTH, nb, 6 * D_MODEL), F32),
        compiler_params=pltpu.CompilerParams(dimension_semantics=("arbitrary", "arbitrary"),
                                             vmem_limit_bytes=VMEM_LIMIT),
        name="adaln_mod",
    )(c_all, ada_w, ada_b.reshape(DEPTH, 1, 6 * D_MODEL))


def _s5_prep_kernel(are_ref, aim_ref, ldt_ref, btr_ref, bti_ref, cr_ref, ci_ref, arow_ref, e_ref, gm_ref,
                    b_out, c_out, l1_out, pp_out, ps_out):
    a_re, a_im = are_ref[0], aim_ref[0]
    dt = jnp.exp(ldt_ref[0])
    mag = jnp.exp(dt * a_re)
    ab_re, ab_im = mag * jnp.cos(dt * a_im), mag * jnp.sin(dt * a_im)
    den = a_re * a_re + a_im * a_im
    f_re = ((ab_re - 1.0) * a_re + ab_im * a_im) / den
    f_im = (ab_im * a_re - (ab_re - 1.0) * a_im) / den
    bb_re = f_re * btr_ref[0] - f_im * bti_ref[0]
    bb_im = f_re * bti_ref[0] + f_im * btr_ref[0]
    e01, gm = e_ref[...], gm_ref[...]
    b_out[0, 0] = (_dot_x_exact01(bb_re, e01) * gm).astype(BF16)
    b_out[0, 1] = (_dot_x_exact01(bb_im, e01) * gm).astype(BF16)
    c_out[0, 0] = (_dot_x_exact01(cr_ref[0], e01) * gm).astype(BF16)
    c_out[0, 1] = (_dot_x_exact01(ci_ref[0], e01) * gm).astype(BF16)
    ar, ai, dtr = arow_ref[0, 0:1, :], arow_ref[0, 1:2, :], jnp.exp(arow_ref[0, 2:3, :])
    row = lax.broadcasted_iota(jnp.int32, (SUBLANES, 1), 0)

    def power(k):
        m = jnp.exp(k * dtr * ar)
        return m * jnp.cos(k * dtr * ai), m * jnp.sin(k * dtr * ai)

    for i, d in enumerate((1, 2, 4)):
        pr, pi = power(jnp.full((SUBLANES, 1), float(d), F32))
        keep = row >= d
        l1_out[0, i] = jnp.where(keep, pr, 0.0)
        l1_out[0, 3 + i] = jnp.where(keep, pi, 0.0)
    pr, pi = power((row + 1).astype(F32))
    pp_out[0, 0], pp_out[0, 1] = pr, pi
    pr, pi = power(jnp.maximum(row - (SLOT_OFF - 1), 0).astype(F32))
    ps_out[0, 0] = jnp.where(row >= SLOT_OFF, pr, 0.0)
    ps_out[0, 1] = jnp.where(row >= SLOT_OFF, pi, 0.0)


def _s5_prepare(s5_a_re, s5_a_im, s5_log_dt, s5_b_re, s5_b_im, s5_c_re, s5_c_im):
    rows = S5_GROUPS * S5_GROUP_CH
    rep = lambda a: jnp.repeat(a, S5_GROUP_CH, axis=1)
    ldt = jnp.broadcast_to(s5_log_dt[:, :, None], (DEPTH, S5_GROUPS, S5_STATE))
    bt = lambda b: jnp.swapaxes(b, 2, 3).reshape(DEPTH, rows, S5_STATE)
    arow = jnp.stack([s5_a_re.reshape(DEPTH, S5_WIDTH), s5_a_im.reshape(DEPTH, S5_WIDTH),
                      ldt.reshape(DEPTH, S5_WIDTH)], axis=1)
    arow = jnp.pad(arow, ((0, 0), (0, SUBLANES - 3), (0, 0)))
    e01 = np.zeros((S5_STATE, S5_WIDTH), np.float32)
    e01[np.arange(S5_WIDTH) % S5_STATE, np.arange(S5_WIDTH)] = 1.0
    gm = (np.arange(rows)[:, None] // S5_GROUP_CH == np.arange(S5_WIDTH)[None, :] // S5_STATE).astype(np.float32)
    p3 = lambda: pl.BlockSpec((1, rows, S5_STATE), lambda l: (l, 0, 0))
    tab = lambda n: pl.BlockSpec((1, n, SUBLANES, S5_WIDTH), lambda l: (l, 0, 0, 0))
    return pl.pallas_call(
        _s5_prep_kernel,
        grid=(DEPTH,),
        in_specs=[p3(), p3(), p3(), p3(), p3(), p3(), p3(),
                  pl.BlockSpec((1, SUBLANES, S5_WIDTH), lambda l: (l, 0, 0)),
                  pl.BlockSpec((S5_STATE, S5_WIDTH), lambda l: (0, 0)),
                  pl.BlockSpec((rows, S5_WIDTH), lambda l: (0, 0))],
        out_specs=[pl.BlockSpec((1, 2, rows, S5_WIDTH), lambda l: (l, 0, 0, 0)),
                   pl.BlockSpec((1, 2, rows, S5_WIDTH), lambda l: (l, 0, 0, 0)),
                   tab(6), tab(2), tab(2)],
        out_shape=[jax.ShapeDtypeStruct((DEPTH, 2, rows, S5_WIDTH), BF16),
                   jax.ShapeDtypeStruct((DEPTH, 2, rows, S5_WIDTH), BF16),
                   jax.ShapeDtypeStruct((DEPTH, 6, SUBLANES, S5_WIDTH), F32),
                   jax.ShapeDtypeStruct((DEPTH, 2, SUBLANES, S5_WIDTH), F32),
                   jax.ShapeDtypeStruct((DEPTH, 2, SUBLANES, S5_WIDTH), F32)],
        compiler_params=pltpu.CompilerParams(dimension_semantics=("arbitrary",), vmem_limit_bytes=VMEM_LIMIT),
        name="s5_prepare",
    )(rep(s5_a_re), rep(s5_a_im), rep(ldt), bt(s5_b_re), bt(s5_b_im),
      s5_c_re.reshape(DEPTH, rows, S5_STATE), s5_c_im.reshape(DEPTH, rows, S5_STATE),
      arow, jnp.asarray(e01, BF16), jnp.asarray(gm, F32))


def _inproj_kernel(x_ref, sh_ref, sc_ref, g_ref, w_ref, o_ref, *, sample):
    x = x_ref[...]
    hn = _rms(x, g_ref[...])
    if sample:
        sh, sc = _expand_rows(sh_ref[...], SLOT), _expand_rows(sc_ref[...], SLOT)
    else:
        sh, sc = sh_ref[0], sc_ref[0]
    h = (hn * (1.0 + sc) + sh).astype(BF16)
    nc = 640
    for j in range(0, NZ, nc):
        o_ref[:, j:j + nc] = jnp.dot(h, w_ref[:, j:j + nc], preferred_element_type=F32)


def _mod_specs(mod_l, chunk_ids, tm, tiles_per_seq, sample):
    if sample:
        return ([pl.BlockSpec((tm // SLOT, D_MODEL), (lambda i, j=j: (i, j))) for j in chunk_ids],
                [mod_l] * len(chunk_ids))
    m3 = mod_l.reshape(mod_l.shape[0] * 6, 1, D_MODEL)
    return ([pl.BlockSpec((1, 1, D_MODEL), (lambda i, j=j: ((i // tiles_per_seq) * 6 + j, 0, 0)))
             for j in chunk_ids], [m3] * len(chunk_ids))


def _const_spec(shape):
    nd = len(shape)
    return pl.BlockSpec(shape, lambda i, _nd=nd: (0,) * _nd, pipeline_mode=pl.Buffered(1))


def _input_projection(x2d, mod_l, gain, w_bf16, *, rows_per_seq, sample):
    t = x2d.shape[0]
    tm = min(TILE_TOKENS, t)
    tiles_per_seq = max(rows_per_seq // tm, 1)
    mspecs, mops = _mod_specs(mod_l, (0, 1), tm, tiles_per_seq, sample)
    return pl.pallas_call(
        functools.partial(_inproj_kernel, sample=sample),
        grid=(t // tm,),
        in_specs=[pl.BlockSpec((tm, D_MODEL), lambda i: (i, 0))] + mspecs
                 + [_const_spec((1, D_MODEL)), _const_spec((D_MODEL, NZ))],
        out_specs=pl.BlockSpec((tm, NZ), lambda i: (i, 0)),
        out_shape=jax.ShapeDtypeStruct((t, NZ), F32),
        compiler_params=pltpu.CompilerParams(dimension_semantics=("arbitrary",), vmem_limit_bytes=VMEM_LIMIT),
        name="in_projection",
    )(x2d, *mops, gain.reshape(1, D_MODEL), w_bf16)


def _causal_conv(ext_ref, xin, w_ref, width, tl):
    ext_ref[pl.ds(SUBLANES, tl), :] = xin
    acc = None
    for k in range(width):
        term = w_ref[k:k + 1, :] * ext_ref[pl.ds(SUBLANES - (width - 1) + k, tl), :]
        acc = term if acc is None else acc + term
    return acc


def _head_meansq(x, e01, width):
    return _dot_x_exact01(x * x, e01) * (1.0 / width)


def _mixers_kernel(*refs, tl, sample):
    (zin, psm, ssd_cw, ssd_cb, ssd_nrm, s5_b, s5_ct, s5_l1, s5_p, s5_d, glu_w, s5_nrm,
     gdn_cw, gdn_nrm, gla_w2, gla_bg, gla_nrm, e01) = refs[:18]
    n_in = 18
    if sample:
        st_ssd, hist_ssd, st_s5r, st_s5i, st_gdn, hist_gdn, st_gla = refs[18:25]
        n_in = 25
    outs = refs[n_in:n_in + 8]
    ymix, o_ssd, o_ssd_cv, o_s5r, o_s5i, o_gdn, o_gdn_cv, o_gla = outs
    (ext_ssd, ext_gdn, s_xbc, s_dt, s_la, s_beta, s_qkv, s_lg, s_y, s_hre, s_him, w_scr) = refs[n_in + 8:]

    rows = lax.broadcasted_iota(jnp.int32, (tl, 1), 0)
    valid = (rows & (SLOT - 1)) >= SLOT_OFF if sample else None
    first = None if sample else (pl.program_id(1) == 0)

    def masked(x):
        return jnp.where(valid, x, 0.0) if sample else x

    if sample:
        ext_ssd[0:SUBLANES, :] = jnp.zeros((SUBLANES, SSD_CONV_DIM), F32)
        ext_gdn[0:SUBLANES, :] = jnp.zeros((SUBLANES, GDN_CONV_DIM), F32)
    else:
        @pl.when(first)
        def _():
            ext_ssd[0:SUBLANES, :] = jnp.zeros((SUBLANES, SSD_CONV_DIM), F32)
            ext_gdn[0:SUBLANES, :] = jnp.zeros((SUBLANES, GDN_CONV_DIM), F32)
            o_ssd[...] = jnp.zeros(o_ssd.shape, F32)
            o_gdn[...] = jnp.zeros(o_gdn.shape, F32)
            w_scr[...] = jnp.zeros(w_scr.shape, F32)
            o_s5r[...] = jnp.zeros(o_s5r.shape, F32)
            o_s5i[...] = jnp.zeros(o_s5i.shape, F32)

    def conv_in(ext_ref, col, hist_ref, w_ref, o_cv):
        xin = zin[:, col:col + 768]
        if sample:
            xin = jnp.where(valid, xin, hist_ref[...])
        y = _causal_conv(ext_ref, xin, w_ref, CONV_W, tl)
        if sample:
            o_cv[...] = xin
        else:
            tail = ext_ref[pl.ds(tl + SUBLANES - (CONV_W - 1), CONV_W - 1), :]
            o_cv[0] = tail
            ext_ref[pl.ds(SUBLANES - (CONV_W - 1), CONV_W - 1), :] = tail
        return y

    s_xbc[...] = _silu(conv_in(ext_ssd, Z_XBC, hist_ssd if sample else None, ssd_cw, o_ssd_cv) + ssd_cb[...])

    sm = zin[:, Z_SM:Z_SM + 128]
    sp = _softplus(sm + psm[0:1, :])
    s_dt[...] = masked(sp)
    s_la[...] = masked(-jnp.exp(psm[1:2, :]) * sp)
    s_beta[...] = masked(_sigmoid(sm))
    s_lg[...] = masked(-_softplus(-(_dot(sm, gla_w2[...]) + gla_bg[...])) * (1.0 / GLA_TAU))

    qkv = _silu(conv_in(ext_gdn, Z_QKV, hist_gdn if sample else None, gdn_cw, o_gdn_cv))
    qk = qkv[:, 0:512]
    inv = lax.rsqrt(_dot_x_exact01(qk * qk, e01[...]) + EPS)
    s_qkv[:, 0:256] = qk[:, 0:256] * inv[:, 0:256] * (GDN_HEADDIM ** -0.5)
    s_qkv[:, 256:512] = qk[:, 256:512] * inv[:, 256:512]
    s_qkv[:, 512:768] = qkv[:, 512:768]

    u = zin[:, Z_U:Z_U + 256]
    n8 = tl // SUBLANES
    x_re = masked(_dot(u, s5_b[0])).reshape(n8, SUBLANES, S5_WIDTH)
    x_im = masked(_dot(u, s5_b[1])).reshape(n8, SUBLANES, S5_WIDTH)
    for i, d in enumerate((1, 2, 4)):
        sr, si = pltpu.roll(x_re, d, 1), pltpu.roll(x_im, d, 1)
        ar, ai = s5_l1[i][None], s5_l1[3 + i][None]
        x_re, x_im = x_re + ar * sr - ai * si, x_im + ar * si + ai * sr
    p_re, p_im = s5_p[0], s5_p[1]
    if sample:
        c_re, c_im = st_s5r[...][:, None, :], st_s5i[...][:, None, :]
        h_re = (x_re + p_re[None] * c_re - p_im[None] * c_im).reshape(tl, S5_WIDTH)
        h_im = (x_im + p_re[None] * c_im + p_im[None] * c_re).reshape(tl, S5_WIDTH)
        o_s5r[...] = h_re
        o_s5i[...] = h_im
    else:
        s_hre[...] = x_re.reshape(tl, S5_WIDTH)
        s_him[...] = x_im.reshape(tl, S5_WIDTH)

        def s5_body(j, carry):
            c_re, c_im = carry
            rs = pl.ds(pl.multiple_of(j * SUBLANES, SUBLANES), SUBLANES)
            hr = s_hre[rs, :] + p_re * c_re - p_im * c_im
            hi = s_him[rs, :] + p_re * c_im + p_im * c_re
            s_hre[rs, :] = hr
            s_him[rs, :] = hi
            return hr[SUBLANES - 1:SUBLANES, :], hi[SUBLANES - 1:SUBLANES, :]

        c_re, c_im = lax.fori_loop(0, n8, s5_body, (o_s5r[0], o_s5i[0]))
        o_s5r[0] = c_re
        o_s5i[0] = c_im
        h_re, h_im = s_hre[...], s_him[...]
    y5 = _dot_nt(h_re, s5_ct[0]) - _dot_nt(h_im, s5_ct[1]) + s5_d[...] * u
    yy = _dot(y5, glu_w[...])
    y5 = _rms(yy[:, 0:256] * _sigmoid(yy[:, 256:512]), s5_nrm[...])
    ymix[:, 256:512] = masked(y5)

    c = SCAN_CHUNK
    groups = range(tl // c)
    heads = range(4)
    pairs = [(i, h) for i in groups for h in heads]
    spg = c // SLOT
    r = lax.broadcasted_iota(jnp.int32, (c, c), 0)
    s = lax.broadcasted_iota(jnp.int32, (c, c), 1)
    if sample:
        same = (r >> 3) == (s >> 3)
        lower, strict, upper = (s <= r) & same, (s < r) & same, (r <= s) & same
    else:
        lower, strict, upper = s <= r, s < r, r <= s
    lower01, upper01 = lower.astype(BF16), upper.astype(BF16)
    grp_rows = [pl.ds(i * c, c) for i in groups]

    def expand(x):
        n = x.shape[1]
        rr = lax.broadcasted_iota(jnp.int32, (c, spg * n), 0) >> 3
        cc = lax.broadcasted_iota(jnp.int32, (c, spg * n), 1) >> (n.bit_length() - 1)
        return jnp.where(rr == cc, jnp.concatenate([x] * spg, axis=1), 0.0)

    la = [s_la[rs, :] for rs in grp_rows]
    cum = [_dot_exact01(lower01, x) for x in la]
    cum_t = [_dot_exact01_tn(x, upper01) for x in la]
    if sample:
        same01 = same.astype(BF16)
        cl = [_dot_exact01(same01, x) for x in la]
    else:
        cl = [x[c - 1:c, :] for x in cum]
    tail = [jnp.exp(a_ - b_) for a_, b_ in zip(cl, cum)]
    ecum = [jnp.exp(x) for x in cum]
    ecl = [jnp.exp(x) for x in cl]

    def col(x, ln):
        return x[:, ln:ln + 1]

    def dec_of(i, ln):
        return jnp.where(lower, jnp.exp(col(cum[i], ln) - cum_t[i][ln:ln + 1, :]), 0.0)

    xbc = [s_xbc[rs, :] for rs in grp_rows]
    dt = [s_dt[rs, :] for rs in grp_rows]
    bgs = [[xbc[i][:, 256 + g * SSD_STATE:256 + (g + 1) * SSD_STATE] for g in range(2)] for i in groups]
    cgs = [[xbc[i][:, 512 + g * SSD_STATE:512 + (g + 1) * SSD_STATE] for g in range(2)] for i in groups]
    cb = [[_dot_nt(cgs[i][g], bgs[i][g]) for g in range(2)] for i in groups]
    xh = {(i, h): xbc[i][:, h * SSD_HEADDIM:(h + 1) * SSD_HEADDIM] for i, h in pairs}
    xq = {p: xh[p] * col(dt[p[0]], SM_DT + p[1]) for p in pairs}
    y_ssd = {(i, h): _dot(cb[i][h // 2] * dec_of(i, SM_DT + h), xq[i, h]) + psm[2:3, h:h + 1] * xh[i, h]
             for i, h in pairs}
    bt = {(i, h): bgs[i][h // 2] * col(tail[i], SM_DT + h) for i, h in pairs}
    cq = {(i, h): cgs[i][h // 2] * col(ecum[i], SM_DT + h) for i, h in pairs}
    upd = {p: _dot_tn(expand(bt[p]) if sample else bt[p], xq[p]) for p in pairs}

    qkv = [s_qkv[rs, :] for rs in grp_rows]
    beta = [s_beta[rs, :] for rs in grp_rows]
    hd = GDN_HEADDIM
    gq_ = {(i, h): qkv[i][:, h * hd:(h + 1) * hd] for i, h in pairs}
    gk = {(i, h): qkv[i][:, 256 + h * hd:256 + (h + 1) * hd] for i, h in pairs}
    gv = {(i, h): qkv[i][:, 512 + h * hd:512 + (h + 1) * hd] for i, h in pairs}
    gdec = {(i, h): dec_of(i, SM_GA + h) for i, h in pairs}
    gb = {(i, h): col(beta[i], SM_GB + h) for i, h in pairs}
    kk = {p: _dot_nt(gk[p], gk[p]) for p in pairs}
    qk_d = {p: _dot_nt(gq_[p], gk[p]) * gdec[p] for p in pairs}
    a_mat = {p: jnp.where(strict, gdec[p] * kk[p], 0.0) * gb[p] for p in pairs}
    t_inv = dict(zip(pairs, _unit_lower_inverse([a_mat[p] for p in pairs], c,
                                                blk=SLOT if sample else 2 * SUBLANES, merge=not sample)))
    ge = {(i, h): col(ecum[i], SM_GA + h) for i, h in pairs}
    rhs = {p: jnp.concatenate([gv[p] * gb[p], gk[p] * (gb[p] * ge[p])], axis=1) for p in pairs}
    x = {p: _dot(t_inv[p], rhs[p]) for p in pairs}
    ktl = {(i, h): gk[i, h] * col(tail[i], SM_GA + h) for i, h in pairs}
    gqe = {p: gq_[p] * ge[p] for p in pairs}

    if sample:
        slots = [pl.ds(i * spg, spg) for i in groups]
        last = [ecl[i].reshape(spg, SLOT, 128)[:, SLOT - 1:SLOT, :] for i in groups]
        s_all = {(i, h): st_gdn[slots[i], h] for i, h in pairs}
        m = {p: jnp.concatenate([expand(x[p][:, hd:2 * hd]), expand(gqe[p])], axis=0) for p in pairs}
        rr = {p: _dot(m[p], s_all[p].reshape(spg * hd, hd)) for p in pairs}
        delta = {p: x[p][:, 0:hd] - rr[p][0:c] for p in pairs}
        og = {p: rr[p][c:2 * c] + _dot(qk_d[p], delta[p]) for p in pairs}
        un = {p: _dot_tn(expand(ktl[p]), delta[p]) for p in pairs}
        h_all = {(i, h): st_ssd[slots[i], h] for i, h in pairs}
        yi = {p: _dot(expand(cq[p]), h_all[p].reshape(spg * SSD_STATE, SSD_HEADDIM)) for p in pairs}
        for i, h in pairs:
            rs = grp_rows[i]
            o_gdn[slots[i], h] = (s_all[i, h] * last[i][:, :, SM_GA + h:SM_GA + h + 1]
                                  + un[i, h].reshape(spg, hd, hd))
            s_y[rs, 512 + h * hd:512 + (h + 1) * hd] = og[i, h]
            o_ssd[slots[i], h] = (h_all[i, h] * last[i][:, :, SM_DT + h:SM_DT + h + 1]
                                  + upd[i, h].reshape(spg, SSD_STATE, SSD_HEADDIM))
            s_y[rs, h * SSD_HEADDIM:(h + 1) * SSD_HEADDIM] = y_ssd[i, h] + yi[i, h]
    else:
        fg = {p: _dot_tn(ktl[p], x[p]) for p in pairs}
        qo = {p: _dot(qk_d[p], x[p]) for p in pairs}
        for i in groups:
            rs = grp_rows[i]
            s0 = [o_gdn[0, h] for h in heads]
            h0 = [o_ssd[0, h] for h in heads]
            m = [jnp.concatenate([gqe[i, h] - qo[i, h][:, hd:2 * hd], fg[i, h][:, hd:2 * hd]], axis=0) for h in heads]
            rr = [_dot(m[h], s0[h]) for h in heads]
            yi = [_dot(cq[i, h], h0[h]) for h in heads]
            for h in heads:
                o_gdn[0, h] = s0[h] * col(ecl[i], SM_GA + h) - rr[h][c:c + hd] + fg[i, h][:, 0:hd]
                s_y[rs, 512 + h * hd:512 + (h + 1) * hd] = rr[h][0:c] + qo[i, h][:, 0:hd]
                o_ssd[0, h] = h0[h] * col(ecl[i], SM_DT + h) + upd[i, h]
                s_y[rs, h * SSD_HEADDIM:(h + 1) * SSD_HEADDIM] = y_ssd[i, h] + yi[h]

    cs = SLOT if sample else GLA_CHUNK
    n_sub, sh = tl // cs, cs.bit_length() - 1
    rt = lax.broadcasted_iota(jnp.int32, (tl, tl), 0)
    ct = lax.broadcasted_iota(jnp.int32, (tl, tl), 1)
    sub_same = (rt >> sh) == (ct >> sh)
    m_cum = sub_same & (ct <= rt)
    m_ref = sub_same & ((ct & (cs - 1)) < cs // 2)
    m3 = jnp.concatenate([m_cum.astype(BF16), m_ref.astype(BF16), sub_same.astype(BF16)], axis=0)
    lg = s_lg[...]
    c3 = _dot_exact01(m3, lg)
    cumg, refg, clg = c3[0:tl], c3[tl:2 * tl], c3[2 * tl:3 * tl]
    gla_q = zin[:, Z_Q:Z_Q + 128] * (GLA_DK ** -0.5)
    gla_k = masked(zin[:, Z_K:Z_K + 128])
    gla_v = zin[:, Z_V:Z_V + 256]
    qe, ke = gla_q * jnp.exp(cumg - refg), gla_k * jnp.exp(refg - cumg)
    qd, kt = gla_q * jnp.exp(cumg), gla_k * jnp.exp(clg - cumg)
    klane = lax.broadcasted_iota(jnp.int32, (1, 128), 1) >> 5
    vlane = lax.broadcasted_iota(jnp.int32, (1, 256), 1) >> 6
    sc = [jnp.where(m_cum, _dot_nt(jnp.where(klane == h, qe, 0.0), ke), 0.0) for h in heads]
    of = [_dot(sc[h], gla_v) for h in heads]
    o_gl = jnp.where(vlane == 0, of[0], 0.0)
    for h in range(1, GLA_HEADS):
        o_gl = o_gl + jnp.where(vlane == h, of[h], 0.0)

    wide = n_sub * 128

    def expand_sub(x):
        rr_ = lax.broadcasted_iota(jnp.int32, (tl, wide), 0) >> sh
        cc_ = lax.broadcasted_iota(jnp.int32, (tl, wide), 1) >> 7
        return jnp.where(rr_ == cc_, jnp.concatenate([x] * n_sub, axis=1), 0.0)

    head_diag = (((lax.broadcasted_iota(jnp.int32, (wide, 256), 0) >> 5) & 3)
                 == (lax.broadcasted_iota(jnp.int32, (wide, 256), 1) >> 6))
    u_all = jnp.where(head_diag, _dot_tn(expand_sub(kt), gla_v), 0.0)
    msub = ((lax.broadcasted_iota(jnp.int32, (tl, n_sub), 0) >> sh)
            == lax.broadcasted_iota(jnp.int32, (tl, n_sub), 1)).astype(BF16)
    dcol = jnp.exp(_dot_exact01_tn(lg, msub))
    if sample:
        w2 = st_gla[...].reshape(n_sub * 128, GLA_DV)
        e4 = ((lax.broadcasted_iota(jnp.int32, (GLA_DV, 256), 1) & (GLA_DV - 1))
              == lax.broadcasted_iota(jnp.int32, (GLA_DV, 256), 0)).astype(BF16)
        w_all = jnp.where(head_diag, _dot_x_exact01(w2, e4), 0.0)
    else:
        ws = []
        w = w_scr[...]
        for i in range(n_sub):
            ws.append(w)
            w = w * dcol[:, i:i + 1] + u_all[i * 128:(i + 1) * 128, :]
        w_scr[...] = w
        w_all = jnp.concatenate(ws, axis=0)
        for h in heads:
            o_gla[0, h] = w[h * GLA_DK:(h + 1) * GLA_DK, h * GLA_DV:(h + 1) * GLA_DV]
    s_y[:, 768:1024] = o_gl + _dot(expand_sub(qd), w_all)
    if sample:
        for i in range(n_sub):
            w = w_all[i * 128:(i + 1) * 128, :] * dcol[:, i:i + 1] + u_all[i * 128:(i + 1) * 128, :]
            for h in heads:
                o_gla[i, h] = w[h * GLA_DK:(h + 1) * GLA_DK, h * GLA_DV:(h + 1) * GLA_DV]

    e256 = e01[0:256, 0:256]
    y = s_y[:, 0:256] * _silu(zin[:, Z_Z:Z_Z + 256])
    ymix[:, 0:256] = masked(_rms(y, ssd_nrm[...]))
    o = s_y[:, 512:768]
    o = o * lax.rsqrt(_head_meansq(o, e256, GDN_HEADDIM) + EPS) * gdn_nrm[...]
    ymix[:, 512:768] = masked(o * _silu(zin[:, Z_GZ:Z_GZ + 256]))
    o = s_y[:, 768:1024]
    o = o * lax.rsqrt(_head_meansq(o, e256, GLA_DV) + EPS) * gla_nrm[...]
    ymix[:, 768:1024] = masked(o * _silu(zin[:, Z_GG:Z_GG + 256]))


def _mixers(zin, lp, *, batch, rows_per_seq, sample, states=None):
    t = zin.shape[0]
    tl = min(SAMPLE_MIX_TILE if sample else TILE_TOKENS, t)
    if sample:
        slots = tl // SLOT
        grid = (t // tl,)
        tok = lambda w: pl.BlockSpec((tl, w), lambda i: (i, 0))
        st4 = lambda a, b_, c_: pl.BlockSpec((slots, a, b_, c_), lambda i: (i, 0, 0, 0))
        cmap = lambda nd: (lambda i: (0,) * nd)
    else:
        tiles = rows_per_seq // tl
        grid = (batch, tiles)
        tok = lambda w: pl.BlockSpec((tl, w), lambda b, l: (b * tiles + l, 0))
        st4 = lambda a, b_, c_: pl.BlockSpec((1, a, b_, c_), lambda b, l: (b, 0, 0, 0))
        cmap = lambda nd: (lambda b, l: (0,) * nd)
    const = lambda a: pl.BlockSpec(a.shape, cmap(a.ndim), pipeline_mode=pl.Buffered(1))
    consts = [lp["psmall"], lp["ssd_cw"], lp["ssd_cb"], lp["ssd_nrm"], lp["s5_b"], lp["s5_ct"], lp["s5_l1"],
              lp["s5_ps"] if sample else lp["s5_pp"], lp["s5_d"], lp["glu_w"], lp["s5_nrm"],
              lp["gdn_cw"], lp["gdn_nrm"], lp["gla_w2"], lp["gla_bg"], lp["gla_nrm"], lp["e01"]]
    in_specs = [tok(NZ)] + [const(a) for a in consts]
    operands = [zin] + consts
    nseq = t // SLOT if sample else batch
    if sample:
        in_specs += [st4(SSD_HEADS, SSD_STATE, SSD_HEADDIM), tok(SSD_CONV_DIM),
                     pl.BlockSpec((slots, S5_WIDTH), lambda i: (i, 0)),
                     pl.BlockSpec((slots, S5_WIDTH), lambda i: (i, 0)),
                     st4(GDN_HEADS, GDN_HEADDIM, GDN_HEADDIM), tok(GDN_CONV_DIM),
                     st4(GLA_HEADS, GLA_DK, GLA_DV)]
        operands += list(states)
        cv_spec = tok(768)
        cv_shape = jax.ShapeDtypeStruct((t, 768), F32)
        s5_spec = tok(S5_WIDTH)
        s5_shape = jax.ShapeDtypeStruct((t, S5_WIDTH), F32)
    else:
        cv_spec = pl.BlockSpec((1, CONV_W - 1, 768), lambda b, l: (b, 0, 0))
        cv_shape = jax.ShapeDtypeStruct((batch, CONV_W - 1, 768), F32)
        s5_spec = pl.BlockSpec((1, 1, S5_WIDTH), lambda b, l: (b, 0, 0))
        s5_shape = jax.ShapeDtypeStruct((batch, 1, S5_WIDTH), F32)
    out_specs = [tok(D_MODEL), st4(SSD_HEADS, SSD_STATE, SSD_HEADDIM), cv_spec, s5_spec, s5_spec,
                 st4(GDN_HEADS, GDN_HEADDIM, GDN_HEADDIM), cv_spec, st4(GLA_HEADS, GLA_DK, GLA_DV)]
    out_shape = [jax.ShapeDtypeStruct((t, D_MODEL), F32),
                 jax.ShapeDtypeStruct((nseq, SSD_HEADS, SSD_STATE, SSD_HEADDIM), F32), cv_shape, s5_shape, s5_shape,
                 jax.ShapeDtypeStruct((nseq, GDN_HEADS, GDN_HEADDIM, GDN_HEADDIM), F32), cv_shape,
                 jax.ShapeDtypeStruct((nseq, GLA_HEADS, GLA_DK, GLA_DV), F32)]
    scratch = [pltpu.VMEM((tl + SUBLANES, 768), F32), pltpu.VMEM((tl + SUBLANES, 768), F32),
               pltpu.VMEM((tl, 768), F32), pltpu.VMEM((tl, 128), F32), pltpu.VMEM((tl, 128), F32),
               pltpu.VMEM((tl, 128), F32), pltpu.VMEM((tl, 768), F32), pltpu.VMEM((tl, 128), F32),
               pltpu.VMEM((tl, D_MODEL), F32), pltpu.VMEM((tl, S5_WIDTH), F32), pltpu.VMEM((tl, S5_WIDTH), F32),
               pltpu.VMEM((GLA_HEADS * GLA_DK, GLA_HEADS * GLA_DV), F32)]
    return pl.pallas_call(
        functools.partial(_mixers_kernel, tl=tl, sample=sample),
        grid=grid, in_specs=in_specs, out_specs=out_specs, out_shape=out_shape, scratch_shapes=scratch,
        compiler_params=pltpu.CompilerParams(dimension_semantics=("arbitrary",) * len(grid),
                                             vmem_limit_bytes=VMEM_LIMIT),
        name="mixers_sample" if sample else "mixers_prompt",
    )(*operands)


def _outffn_kernel(*refs, tm, sample, final):
    x_ref, y_ref, g1_ref, sh_ref, sc_ref, g2_ref, nf_ref, wout, up, cw, cb, down = refs[:12]
    k = 12
    hist = fin = None
    if sample:
        hist = refs[k]
        k += 1
    if final:
        fin = refs[k]
        k += 1
    x2_ref, cv_ref, ext = refs[k], refs[k + 1], refs[k + 2]

    if sample:
        ex = lambda r: _expand_rows(r[...], SLOT)
        g1, sh, sc, g2 = ex(g1_ref), ex(sh_ref), ex(sc_ref), ex(g2_ref)
        valid = (lax.broadcasted_iota(jnp.int32, (tm, 1), 0) & (SLOT - 1)) >= SLOT_OFF
        ext[0:SUBLANES, :] = jnp.zeros((SUBLANES, 2 * D_FF), F32)
    else:
        g1, sh, sc, g2 = g1_ref[0], sh_ref[0], sc_ref[0], g2_ref[0]

        @pl.when(pl.program_id(1) == 0)
        def _():
            ext[0:SUBLANES, :] = jnp.zeros((SUBLANES, 2 * D_FF), F32)

    x1 = x_ref[...] + g1 * _dot(y_ref[...], wout[...])
    h = (_rms(x1, nf_ref[...]) * (1.0 + sc) + sh).astype(BF16)
    for j in range(0, 2 * D_FF, FF_CHUNK):
        uj = jnp.dot(h, up[:, j:j + FF_CHUNK], preferred_element_type=F32)
        if sample:
            uj = jnp.where(valid, uj, hist[:, j:j + FF_CHUNK])
        ext[pl.ds(SUBLANES, tm), j:j + FF_CHUNK] = uj

    def conv(col):
        acc = cb[:, col:col + FF_CHUNK]
        for kk in range(FFN_CONV_W):
            acc = acc + cw[kk:kk + 1, col:col + FF_CHUNK] * ext[pl.ds(SUBLANES - (FFN_CONV_W - 1) + kk, tm),
                                                                 col:col + FF_CHUNK]
        return acc

    f = jnp.zeros((tm, D_MODEL), F32)
    for j in range(0, D_FF, FF_CHUNK):
        act = _silu(conv(j)) * conv(D_FF + j)
        f = f + _dot(act, down[j:j + FF_CHUNK, :])
    x2 = x1 + g2 * f
    if final:
        x2 = _rms(x2, fin[...])
    if sample:
        x2_ref[...] = jnp.where(valid, x2, 0.0)
        cv_ref[...] = ext[pl.ds(SUBLANES, tm), :]
    else:
        x2_ref[...] = x2
        tail = ext[pl.ds(tm + SUBLANES - (FFN_CONV_W - 1), FFN_CONV_W - 1), :]
        cv_ref[0] = tail
        ext[pl.ds(SUBLANES - (FFN_CONV_W - 1), FFN_CONV_W - 1), :] = tail


def _out_ffn(x2d, ymix, mod_l, lp, *, batch, rows_per_seq, sample, hist=None, final_gain=None):
    t = x2d.shape[0]
    tm = min(SAMPLE_FFN_TILE if sample else TILE_TOKENS, t)
    final = final_gain is not None
    if sample:
        grid = (t // tm,)
        tok = lambda w: pl.BlockSpec((tm, w), lambda i: (i, 0))
        cmap = lambda nd: (lambda i: (0,) * nd)
        mspecs = [pl.BlockSpec((tm // SLOT, D_MODEL), (lambda i, j=j: (i, j))) for j in (2, 3, 4, 5)]
        mops = [mod_l] * 4
        cv_spec, cv_shape = tok(2 * D_FF), jax.ShapeDtypeStruct((t, 2 * D_FF), F32)
    else:
        tiles = rows_per_seq // tm
        grid = (batch, tiles)
        tok = lambda w: pl.BlockSpec((tm, w), lambda b, l: (b * tiles + l, 0))
        cmap = lambda nd: (lambda b, l: (0,) * nd)
        m3 = mod_l.reshape(batch * 6, 1, D_MODEL)
        mspecs = [pl.BlockSpec((1, 1, D_MODEL), (lambda b, l, j=j: (b * 6 + j, 0, 0))) for j in (2, 3, 4, 5)]
        mops = [m3] * 4
        cv_spec = pl.BlockSpec((1, FFN_CONV_W - 1, 2 * D_FF), lambda b, l: (b, 0, 0))
        cv_shape = jax.ShapeDtypeStruct((batch, FFN_CONV_W - 1, 2 * D_FF), F32)
    const = lambda a: pl.BlockSpec(a.shape, cmap(a.ndim), pipeline_mode=pl.Buffered(1))
    consts = [lp["norm_ffn"], lp["w_out"], lp["ffn_up"], lp["ffn_cw"], lp["ffn_cb"], lp["ffn_down"]]
    in_specs = [tok(D_MODEL), tok(D_MODEL)] + mspecs + [const(a) for a in consts]
    operands = [x2d, ymix] + mops + consts
    if sample:
        in_specs.append(tok(2 * D_FF))
        operands.append(hist)
    if final:
        in_specs.append(const(final_gain))
        operands.append(final_gain)
    return pl.pallas_call(
        functools.partial(_outffn_kernel, tm=tm, sample=sample, final=final),
        grid=grid, in_specs=in_specs,
        out_specs=[tok(D_MODEL), cv_spec],
        out_shape=[jax.ShapeDtypeStruct((t, D_MODEL), F32), cv_shape],
        scratch_shapes=[pltpu.VMEM((tm + SUBLANES, 2 * D_FF), F32)],
        compiler_params=pltpu.CompilerParams(dimension_semantics=("arbitrary",) * len(grid),
                                             vmem_limit_bytes=VMEM_LIMIT),
        name="out_ffn_sample" if sample else "out_ffn_prompt",
    )(*operands)


def _reorder_w_in(w_in):
    o = np.cumsum([0, 256, 768, 4, 256, 768, 256, 4, 4, 128, 128, 256, 256, 16])
    seg = lambda i: w_in[:, :, o[i]:o[i + 1]]
    small = [seg(2), seg(6), seg(7), seg(12)]
    pad = jnp.zeros(w_in.shape[:2] + (128 - 28,), w_in.dtype)
    cols = [seg(1), seg(4), seg(0), seg(3), seg(5), seg(10), seg(11), seg(8), seg(9)] + small + [pad]
    return jnp.concatenate(cols, axis=-1).astype(BF16)


def _pad_lanes(a, lane0, width=128):
    return jnp.pad(a, ((0, 0), (lane0, width - lane0 - a.shape[1])))[:, None, :]


def _slot_rows(a, first_row):
    b, r, c = a.shape
    return jnp.pad(a, ((0, 0), (first_row, SLOT - first_row - r), (0, 0))).reshape(b * SLOT, c)


def kernel(x_prompt, x_sample, c_prompt, c_sample, state_ssd, state_ssd_conv, state_s5_re, state_s5_im,
           state_gdn, state_gdn_conv, state_gla, state_ffn_conv, ada_w, ada_b, norm_mix, norm_ffn, w_in, w_out,
           ssd_conv_w, ssd_conv_b, ssd_dt_bias, ssd_a_log, ssd_d, ssd_norm, s5_a_re, s5_a_im, s5_log_dt,
           s5_b_re, s5_b_im, s5_c_re, s5_c_im, s5_d, s5_glu_w, s5_norm, gdn_conv_w, gdn_a_log, gdn_dt_bias,
           gdn_norm, gla_wg2, gla_bg, gla_norm, ffn_up, ffn_conv_w, ffn_conv_b, ffn_down, final_norm):
    nb, seq = x_prompt.shape[0], x_prompt.shape[1]
    ns, dseq = x_sample.shape[0], x_sample.shape[1]
    assert dseq == SLOT - SLOT_OFF

    mod = _modulation(jnp.concatenate([c_prompt, c_sample], axis=0).astype(F32), ada_w, ada_b)
    s5_b, s5_ct, s5_l1, s5_pp, s5_ps = _s5_prepare(s5_a_re, s5_a_im, s5_log_dt, s5_b_re, s5_b_im, s5_c_re, s5_c_im)

    w_in_r = _reorder_w_in(w_in)
    w_out_b, up_b, down_b, glu_b = w_out.astype(BF16), ffn_up.astype(BF16), ffn_down.astype(BF16), s5_glu_w.astype(BF16)
    psmall = jnp.concatenate([
        _pad_lanes(ssd_dt_bias, SM_DT) + _pad_lanes(gdn_dt_bias, SM_GA),
        _pad_lanes(ssd_a_log, SM_DT) + _pad_lanes(gdn_a_log, SM_GA),
        _pad_lanes(ssd_d, 0), jnp.zeros((DEPTH, SUBLANES - 3, 128), F32)], axis=1)
    gla_w2 = jnp.pad(gla_wg2, ((0, 0), (SM_LR, 128 - SM_LR - GLA_GATE_RANK), (0, 0))).astype(BF16)
    e01 = jnp.asarray(np.arange(512)[:, None] // 64 == np.arange(512)[None, :] // 64, BF16)
    row = lambda a: a[:, None, :]

    layers = []
    for l in range(DEPTH):
        layers.append(dict(
            psmall=psmall[l], ssd_cw=ssd_conv_w[l], ssd_cb=row(ssd_conv_b)[l], ssd_nrm=row(ssd_norm)[l],
            s5_b=s5_b[l], s5_ct=s5_ct[l], s5_l1=s5_l1[l], s5_pp=s5_pp[l], s5_ps=s5_ps[l], s5_d=row(s5_d)[l],
            glu_w=glu_b[l], s5_nrm=row(s5_norm)[l], gdn_cw=gdn_conv_w[l],
            gdn_nrm=jnp.tile(gdn_norm[l], GDN_HEADS)[None, :], gla_w2=gla_w2[l], gla_bg=row(gla_bg)[l],
            gla_nrm=jnp.tile(gla_norm[l], GLA_HEADS)[None, :], e01=e01,
            norm_mix=norm_mix[l], norm_ffn=row(norm_ffn)[l], w_in=w_in_r[l], w_out=w_out_b[l],
            ffn_up=up_b[l], ffn_cw=ffn_conv_w[l], ffn_cb=row(ffn_conv_b)[l], ffn_down=down_b[l]))
    fin = final_norm.reshape(1, D_MODEL)

    xp = x_prompt.astype(F32).reshape(nb * seq, D_MODEL)
    p_new = []
    for l, lp in enumerate(layers):
        mod_l = mod[l, :nb]
        zin = _input_projection(xp, mod_l, lp["norm_mix"], lp["w_in"], rows_per_seq=seq, sample=False)
        ymix, h_ssd, cv_ssd, h5r, h5i, s_gdn, cv_gdn, s_gla = _mixers(zin, lp, batch=nb, rows_per_seq=seq, sample=False)
        xp, cv_ffn = _out_ffn(xp, ymix, mod_l, lp, batch=nb, rows_per_seq=seq, sample=False,
                              final_gain=fin if l == DEPTH - 1 else None)
        p_new.append((h_ssd, cv_ssd, h5r.reshape(nb, S5_GROUPS, S5_STATE), h5i.reshape(nb, S5_GROUPS, S5_STATE),
                      s_gdn, cv_gdn, s_gla, cv_ffn))

    xs = jnp.pad(x_sample.astype(F32), ((0, 0), (SLOT_OFF, 0), (0, 0))).reshape(ns * SLOT, D_MODEL)
    s_new = []
    for l, lp in enumerate(layers):
        mod_l = mod[l, nb:]
        states = (state_ssd[l].astype(F32), _slot_rows(state_ssd_conv[l].astype(F32), SLOT_OFF - (CONV_W - 1)),
                  state_s5_re[l].astype(F32).reshape(ns, S5_WIDTH), state_s5_im[l].astype(F32).reshape(ns, S5_WIDTH),
                  state_gdn[l].astype(F32), _slot_rows(state_gdn_conv[l].astype(F32), SLOT_OFF - (CONV_W - 1)),
                  state_gla[l].astype(F32))
        zin = _input_projection(xs, mod_l, lp["norm_mix"], lp["w_in"], rows_per_seq=SLOT, sample=True)
        ymix, h_ssd, cv_ssd, h5r, h5i, s_gdn, cv_gdn, s_gla = _mixers(zin, lp, batch=ns, rows_per_seq=SLOT,
                                                                      sample=True, states=states)
        hist = _slot_rows(state_ffn_conv[l].astype(F32), SLOT_OFF - (FFN_CONV_W - 1))
        xs, cv_ffn = _out_ffn(xs, ymix, mod_l, lp, batch=ns, rows_per_seq=SLOT, sample=True, hist=hist,
                              final_gain=fin if l == DEPTH - 1 else None)
        last = lambda a, n: a.reshape(ns, SLOT, a.shape[-1])[:, SLOT - n:, :]
        s_new.append((h_ssd, last(cv_ssd, CONV_W - 1),
                      last(h5r, 1).reshape(ns, S5_GROUPS, S5_STATE), last(h5i, 1).reshape(ns, S5_GROUPS, S5_STATE),
                      s_gdn, last(cv_gdn, CONV_W - 1), s_gla, last(cv_ffn, FFN_CONV_W - 1)))

    y_p = xp.reshape(nb, seq, D_MODEL).astype(x_prompt.dtype)
    y_s = xs.reshape(ns, SLOT, D_MODEL)[:, SLOT_OFF:, :].astype(x_sample.dtype)
    p_st = [jnp.stack(t) for t in zip(*p_new)]
    s_st = [jnp.stack(t) for t in zip(*s_new)]
    return (y_p, y_s, *p_st, *s_st)
```

```python
import functools

import numpy as np
import jax
import jax.numpy as jnp
from jax import lax
from jax.experimental import pallas as pl
from jax.experimental.pallas import tpu as pltpu

F32 = jnp.float32
BF16 = jnp.bfloat16

D_MODEL = 1024
DEPTH = 2
GROUP_WIDTH = 256
CONV_W = 4
SSD_HEADS = 4
SSD_HEADDIM = 64
SSD_STATE = 128
SSD_CONV_DIM = 768
S5_GROUPS = 16
S5_GROUP_CH = 16
S5_STATE = 64
S5_WIDTH = S5_GROUPS * S5_STATE
GDN_HEADS = 4
GDN_HEADDIM = 64
GDN_CONV_DIM = 768
GLA_HEADS = 4
GLA_DK = 32
GLA_DV = 64
GLA_GATE_RANK = 16
GLA_TAU = 16.0
D_FF = 2816
FFN_CONV_W = 3
EPS = 1e-6

NZ = 3200
Z_XBC, Z_QKV, Z_Z, Z_U, Z_GZ, Z_V, Z_GG, Z_Q, Z_K, Z_SM = 0, 768, 1536, 1792, 2048, 2304, 2560, 2816, 2944, 3072
SM_DT, SM_GA, SM_GB, SM_LR = 0, 4, 8, 12

SUBLANES = 8
SLOT = SUBLANES
SLOT_OFF = SLOT - 4
SCAN_CHUNK = 64
GLA_CHUNK = 16
TILE_TOKENS = 256
SAMPLE_MIX_TILE = 128
SAMPLE_FFN_TILE = 128
FF_CHUNK = 256
VMEM_LIMIT = 56 * 1024 * 1024


def _silu(x):
    return x * (1.0 / (1.0 + jnp.exp(-x)))


def _sigmoid(x):
    return 1.0 / (1.0 + jnp.exp(-x))


def _softplus(x):
    return jnp.maximum(x, 0.0) + jnp.log(1.0 + jnp.exp(-jnp.abs(x)))


def _dot(a, b):
    return jnp.dot(a.astype(BF16), b.astype(BF16), preferred_element_type=F32)


def _dot_nt(a, b):
    return lax.dot_general(a.astype(BF16), b.astype(BF16), (((1,), (1,)), ((), ())), preferred_element_type=F32)


def _dot_tn(a, b):
    return lax.dot_general(a.astype(BF16), b.astype(BF16), (((0,), (0,)), ((), ())), preferred_element_type=F32)


def _split_hi_lo(x):
    hi = x.astype(BF16)
    lo = (x - hi.astype(F32)).astype(BF16)
    return hi, lo


def _dot_exact01(m01, x):
    hi, lo = _split_hi_lo(x)
    return jnp.dot(m01, hi, preferred_element_type=F32) + jnp.dot(m01, lo, preferred_element_type=F32)


def _dot_exact01_tn(x, m01):
    hi, lo = _split_hi_lo(x)
    dn = (((0,), (0,)), ((), ()))
    return (lax.dot_general(hi, m01, dn, preferred_element_type=F32)
            + lax.dot_general(lo, m01, dn, preferred_element_type=F32))


def _place_rows(p01, x):
    hi = x.astype(BF16)
    rest = x - hi.astype(F32)
    mid = rest.astype(BF16)
    lo = (rest - mid.astype(F32)).astype(BF16)
    mm = lambda v: jnp.dot(p01, v, preferred_element_type=F32)
    return (mm(hi) + mm(mid)) + mm(lo)


def _dot_x_exact01(x, m01):
    hi, lo = _split_hi_lo(x)
    return jnp.dot(hi, m01, preferred_element_type=F32) + jnp.dot(lo, m01, preferred_element_type=F32)


def _unit_lower_inverse(mats, c, blk, merge):
    r = lax.broadcasted_iota(jnp.int32, (c, c), 0)
    s = lax.broadcasted_iota(jnp.int32, (c, c), 1)
    sh = blk.bit_length() - 1
    diag = (r >> sh) == (s >> sh)
    eye = jnp.where(r == s, 1.0, 0.0)
    p = [jnp.where(diag, a, 0.0) for a in mats]
    t = [eye - d for d in p]
    pw = 2
    while pw < blk:
        p = [_dot(d, d) for d in p]
        t = [m + _dot(m, d) for m, d in zip(t, p)]
        pw *= 2
    size = blk
    while merge and size < c:
        sh = size.bit_length() - 1
        pair = ((r >> (sh + 1)) == (s >> (sh + 1))) & ((r >> sh) != (s >> sh))
        lt = [_dot(jnp.where(pair, a, 0.0), m) for a, m in zip(mats, t)]
        t = [m - _dot(m, y) for m, y in zip(t, lt)]
        size *= 2
    return t


def _rms(x, gain):
    return x * lax.rsqrt(jnp.mean(x * x, axis=-1, keepdims=True) + EPS) * gain


def _expand_rows(m, reps):
    g, n = m.shape
    return jnp.broadcast_to(m[:, None, :], (g, reps, n)).reshape(g * reps, n)


def _mod_kernel(cp_ref, cs_ref, w_ref, b_ref, op_ref, os_ref):
    w = w_ref[0].astype(BF16)
    op_ref[0] = _dot(_silu(cp_ref[...]), w) + b_ref[0]
    os_ref[0] = _dot(_silu(cs_ref[...]), w) + b_ref[0]


def _modulation(c_prompt, c_sample, ada_w, ada_b):
    nb, ns = c_prompt.shape[0], c_sample.shape[0]
    tn = 1024
    return pl.pallas_call(
        _mod_kernel,
        grid=(DEPTH, 6 * D_MODEL // tn),
        in_specs=[pl.BlockSpec((nb, D_MODEL), lambda l, j: (0, 0)),
                  pl.BlockSpec((ns, D_MODEL), lambda l, j: (0, 0)),
                  pl.BlockSpec((1, D_MODEL, tn), lambda l, j: (l, 0, j)),
                  pl.BlockSpec((1, 1, tn), lambda l, j: (l, 0, j))],
        out_specs=[pl.BlockSpec((1, nb, tn), lambda l, j: (l, 0, j)),
                   pl.BlockSpec((1, ns, tn), lambda l, j: (l, 0, j))],
        out_shape=[jax.ShapeDtypeStruct((DEPTH, nb, 6 * D_MODEL), F32),
                   jax.ShapeDtypeStruct((DEPTH, ns, 6 * D_MODEL), F32)],
        compiler_params=pltpu.CompilerParams(dimension_semantics=("arbitrary", "arbitrary"),
                                             vmem_limit_bytes=VMEM_LIMIT),
        name="adaln_mod",
    )(c_prompt, c_sample, ada_w, ada_b.reshape(DEPTH, 1, 6 * D_MODEL))


def _s5_prep_kernel(are_ref, aim_ref, ldt_ref, btr_ref, bti_ref, cr_ref, ci_ref, arow_ref, e_ref, gm_ref,
                    b_out, c_out, l1_out, pp_out, ps_out):
    a_re, a_im = are_ref[0], aim_ref[0]
    dt = jnp.exp(ldt_ref[0])
    mag = jnp.exp(dt * a_re)
    ab_re, ab_im = mag * jnp.cos(dt * a_im), mag * jnp.sin(dt * a_im)
    den = a_re * a_re + a_im * a_im
    f_re = ((ab_re - 1.0) * a_re + ab_im * a_im) / den
    f_im = (ab_im * a_re - (ab_re - 1.0) * a_im) / den
    bb_re = f_re * btr_ref[0] - f_im * bti_ref[0]
    bb_im = f_re * bti_ref[0] + f_im * btr_ref[0]
    e01, gm = e_ref[...], gm_ref[...]
    b_out[0, 0] = (_dot_x_exact01(bb_re, e01) * gm).astype(BF16)
    b_out[0, 1] = (_dot_x_exact01(bb_im, e01) * gm).astype(BF16)
    c_out[0, 0] = (_dot_x_exact01(cr_ref[0], e01) * gm).astype(BF16)
    c_out[0, 1] = (_dot_x_exact01(ci_ref[0], e01) * gm).astype(BF16)
    ar, ai, dtr = arow_ref[0, 0:1, :], arow_ref[0, 1:2, :], jnp.exp(arow_ref[0, 2:3, :])
    row = lax.broadcasted_iota(jnp.int32, (SUBLANES, 1), 0)

    def power(k):
        m = jnp.exp(k * dtr * ar)
        return m * jnp.cos(k * dtr * ai), m * jnp.sin(k * dtr * ai)

    for i, d in enumerate((1, 2, 4)):
        pr, pi = power(jnp.full((SUBLANES, 1), float(d), F32))
        keep = row >= d
        l1_out[0, i] = jnp.where(keep, pr, 0.0)
        l1_out[0, 3 + i] = jnp.where(keep, pi, 0.0)
    pr, pi = power((row + 1).astype(F32))
    pp_out[0, 0], pp_out[0, 1] = pr, pi
    pr, pi = power(jnp.maximum(row - (SLOT_OFF - 1), 0).astype(F32))
    ps_out[0, 0] = jnp.where(row >= SLOT_OFF, pr, 0.0)
    ps_out[0, 1] = jnp.where(row >= SLOT_OFF, pi, 0.0)


def _s5_prepare(s5_a_re, s5_a_im, s5_log_dt, s5_b_re, s5_b_im, s5_c_re, s5_c_im):
    rows = S5_GROUPS * S5_GROUP_CH
    rep = lambda a: jnp.repeat(a, S5_GROUP_CH, axis=1)
    ldt = jnp.broadcast_to(s5_log_dt[:, :, None], (DEPTH, S5_GROUPS, S5_STATE))
    bt = lambda b: jnp.swapaxes(b, 2, 3).reshape(DEPTH, rows, S5_STATE)
    arow = jnp.stack([s5_a_re.reshape(DEPTH, S5_WIDTH), s5_a_im.reshape(DEPTH, S5_WIDTH),
                      ldt.reshape(DEPTH, S5_WIDTH)], axis=1)
    arow = jnp.pad(arow, ((0, 0), (0, SUBLANES - 3), (0, 0)))
    e01 = np.zeros((S5_STATE, S5_WIDTH), np.float32)
    e01[np.arange(S5_WIDTH) % S5_STATE, np.arange(S5_WIDTH)] = 1.0
    gm = (np.arange(rows)[:, None] // S5_GROUP_CH == np.arange(S5_WIDTH)[None, :] // S5_STATE).astype(np.float32)
    p3 = lambda: pl.BlockSpec((1, rows, S5_STATE), lambda l: (l, 0, 0))
    tab = lambda n: pl.BlockSpec((1, n, SUBLANES, S5_WIDTH), lambda l: (l, 0, 0, 0))
    return pl.pallas_call(
        _s5_prep_kernel,
        grid=(DEPTH,),
        in_specs=[p3(), p3(), p3(), p3(), p3(), p3(), p3(),
                  pl.BlockSpec((1, SUBLANES, S5_WIDTH), lambda l: (l, 0, 0)),
                  pl.BlockSpec((S5_STATE, S5_WIDTH), lambda l: (0, 0)),
                  pl.BlockSpec((rows, S5_WIDTH), lambda l: (0, 0))],
        out_specs=[pl.BlockSpec((1, 2, rows, S5_WIDTH), lambda l: (l, 0, 0, 0)),
                   pl.BlockSpec((1, 2, rows, S5_WIDTH), lambda l: (l, 0, 0, 0)),
                   tab(6), tab(2), tab(2)],
        out_shape=[jax.ShapeDtypeStruct((DEPTH, 2, rows, S5_WIDTH), BF16),
                   jax.ShapeDtypeStruct((DEPTH, 2, rows, S5_WIDTH), BF16),
                   jax.ShapeDtypeStruct((DEPTH, 6, SUBLANES, S5_WIDTH), F32),
                   jax.ShapeDtypeStruct((DEPTH, 2, SUBLANES, S5_WIDTH), F32),
                   jax.ShapeDtypeStruct((DEPTH, 2, SUBLANES, S5_WIDTH), F32)],
        compiler_params=pltpu.CompilerParams(dimension_semantics=("arbitrary",), vmem_limit_bytes=VMEM_LIMIT),
        name="s5_prepare",
    )(rep(s5_a_re), rep(s5_a_im), rep(ldt), bt(s5_b_re), bt(s5_b_im),
      s5_c_re.reshape(DEPTH, rows, S5_STATE), s5_c_im.reshape(DEPTH, rows, S5_STATE),
      arow, jnp.asarray(e01, BF16), jnp.asarray(gm, F32))


def _inproj_kernel(x_ref, sh_ref, sc_ref, g_ref, w_ref, o_ref, *, sample):
    x = x_ref[...]
    hn = _rms(x, g_ref[...])
    if sample:
        sh, sc = _expand_rows(sh_ref[...], SLOT), _expand_rows(sc_ref[...], SLOT)
    else:
        sh, sc = sh_ref[0], sc_ref[0]
    h = (hn * (1.0 + sc) + sh).astype(BF16)
    nc = 640
    for j in range(0, NZ, nc):
        o_ref[:, j:j + nc] = jnp.dot(h, w_ref[:, j:j + nc], preferred_element_type=F32)


def _mod_specs(mod, l, chunk_ids, tm, tiles_per_seq, sample):
    if sample:
        return ([pl.BlockSpec((None, tm // SLOT, D_MODEL), (lambda i, j=j: (l, i, j))) for j in chunk_ids],
                [mod] * len(chunk_ids))
    nb = mod.shape[1]
    m3 = mod.reshape(DEPTH * nb * 6, 1, D_MODEL)
    return ([pl.BlockSpec((1, 1, D_MODEL), (lambda i, j=j: ((l * nb + i // tiles_per_seq) * 6 + j, 0, 0)))
             for j in chunk_ids], [m3] * len(chunk_ids))


def _layer_spec(a, l, grid_rank):
    nd = a.ndim - 1
    return pl.BlockSpec((None,) + a.shape[1:], lambda *_: (l,) + (0,) * nd, pipeline_mode=pl.Buffered(1))


def _shared_spec(a):
    nd = a.ndim
    return pl.BlockSpec(a.shape, lambda *_: (0,) * nd, pipeline_mode=pl.Buffered(1))


def _input_projection(x2d, mod, l, prm, *, rows_per_seq, sample):
    t = x2d.shape[0]
    tm = min(TILE_TOKENS, t)
    tiles_per_seq = max(rows_per_seq // tm, 1)
    mspecs, mops = _mod_specs(mod, l, (0, 1), tm, tiles_per_seq, sample)
    return pl.pallas_call(
        functools.partial(_inproj_kernel, sample=sample),
        grid=(t // tm,),
        in_specs=[pl.BlockSpec((tm, D_MODEL), lambda i: (i, 0))] + mspecs
                 + [_layer_spec(prm["norm_mix"], l, 1), _layer_spec(prm["w_in"], l, 1)],
        out_specs=pl.BlockSpec((tm, NZ), lambda i: (i, 0)),
        out_shape=jax.ShapeDtypeStruct((t, NZ), F32),
        compiler_params=pltpu.CompilerParams(dimension_semantics=("arbitrary",), vmem_limit_bytes=VMEM_LIMIT),
        name="in_projection",
    )(x2d, *mops, prm["norm_mix"], prm["w_in"])


def _causal_conv(ext_ref, xin, w_ref, width, tl):
    ext_ref[pl.ds(SUBLANES, tl), :] = xin
    acc = None
    for k in range(width):
        term = w_ref[k:k + 1, :] * ext_ref[pl.ds(SUBLANES - (width - 1) + k, tl), :]
        acc = term if acc is None else acc + term
    return acc


def _head_meansq(x, e01, width):
    return _dot_x_exact01(x * x, e01) * (1.0 / width)


def _mixers_kernel(*refs, tl, sample):
    (zin, psm, ssd_cw, ssd_cb, ssd_nrm, s5_b, s5_ct, s5_l1, s5_p, s5_d, glu_w, s5_nrm,
     gdn_cw, gdn_nrm, gla_w2, gla_bg, gla_nrm, e01) = refs[:18]
    n_in = 18
    if sample:
        st_ssd, hist_ssd, st_s5r, st_s5i, st_gdn, hist_gdn, st_gla, put3, get3, get1 = refs[18:28]
        n_in = 28
    outs = refs[n_in:n_in + 8]
    ymix, o_ssd, o_ssd_cv, o_s5r, o_s5i, o_gdn, o_gdn_cv, o_gla = outs
    (ext_ssd, ext_gdn, s_xbc, s_dt, s_la, s_beta, s_qkv, s_lg, s_y, s_hre, s_him, w_scr) = refs[n_in + 8:]

    rows = lax.broadcasted_iota(jnp.int32, (tl, 1), 0)
    valid = (rows & (SLOT - 1)) >= SLOT_OFF if sample else None
    first = None if sample else (pl.program_id(1) == 0)

    def masked(x):
        return jnp.where(valid, x, 0.0) if sample else x

    if sample:
        ext_ssd[0:SUBLANES, :] = jnp.zeros((SUBLANES, SSD_CONV_DIM), F32)
        ext_gdn[0:SUBLANES, :] = jnp.zeros((SUBLANES, GDN_CONV_DIM), F32)
    else:
        @pl.when(first)
        def _():
            ext_ssd[0:SUBLANES, :] = jnp.zeros((SUBLANES, SSD_CONV_DIM), F32)
            ext_gdn[0:SUBLANES, :] = jnp.zeros((SUBLANES, GDN_CONV_DIM), F32)
            o_ssd[...] = jnp.zeros(o_ssd.shape, F32)
            o_gdn[...] = jnp.zeros(o_gdn.shape, F32)
            w_scr[...] = jnp.zeros(w_scr.shape, F32)
            o_s5r[...] = jnp.zeros(o_s5r.shape, F32)
            o_s5i[...] = jnp.zeros(o_s5i.shape, F32)

    def conv_in(ext_ref, col, hist_ref, w_ref, o_cv):
        xin = zin[:, col:col + 768]
        if sample:
            xin = jnp.where(valid, xin, _place_rows(put3[...], hist_ref[...]))
        y = _causal_conv(ext_ref, xin, w_ref, CONV_W, tl)
        if sample:
            o_cv[...] = _place_rows(get3[...], xin)
        else:
            tail = ext_ref[pl.ds(tl + SUBLANES - (CONV_W - 1), CONV_W - 1), :]
            o_cv[0] = tail
            ext_ref[pl.ds(SUBLANES - (CONV_W - 1), CONV_W - 1), :] = tail
        return y

    s_xbc[...] = _silu(conv_in(ext_ssd, Z_XBC, hist_ssd if sample else None, ssd_cw, o_ssd_cv) + ssd_cb[...])

    sm = zin[:, Z_SM:Z_SM + 128]
    sp = _softplus(sm + psm[0:1, :])
    s_dt[...] = masked(sp)
    s_la[...] = masked(-jnp.exp(psm[1:2, :]) * sp)
    s_beta[...] = masked(_sigmoid(sm))
    s_lg[...] = masked(-_softplus(-(_dot(sm, gla_w2[...]) + gla_bg[...])) * (1.0 / GLA_TAU))

    qkv = _silu(conv_in(ext_gdn, Z_QKV, hist_gdn if sample else None, gdn_cw, o_gdn_cv))
    qk = qkv[:, 0:512]
    inv = lax.rsqrt(_dot_x_exact01(qk * qk, e01[...]) + EPS)
    s_qkv[:, 0:256] = qk[:, 0:256] * inv[:, 0:256] * (GDN_HEADDIM ** -0.5)
    s_qkv[:, 256:512] = qk[:, 256:512] * inv[:, 256:512]
    s_qkv[:, 512:768] = qkv[:, 512:768]

    u = zin[:, Z_U:Z_U + 256]
    n8 = tl // SUBLANES
    x_re = masked(_dot(u, s5_b[0])).reshape(n8, SUBLANES, S5_WIDTH)
    x_im = masked(_dot(u, s5_b[1])).reshape(n8, SUBLANES, S5_WIDTH)
    for i, d in enumerate((1, 2, 4)):
        sr, si = pltpu.roll(x_re, d, 1), pltpu.roll(x_im, d, 1)
        ar, ai = s5_l1[i][None], s5_l1[3 + i][None]
        x_re, x_im = x_re + ar * sr - ai * si, x_im + ar * si + ai * sr
    p_re, p_im = s5_p[0], s5_p[1]
    if sample:
        c_re, c_im = st_s5r[...][:, None, :], st_s5i[...][:, None, :]
        h_re = (x_re + p_re[None] * c_re - p_im[None] * c_im).reshape(tl, S5_WIDTH)
        h_im = (x_im + p_re[None] * c_im + p_im[None] * c_re).reshape(tl, S5_WIDTH)
        o_s5r[...] = _place_rows(get1[...], h_re)
        o_s5i[...] = _place_rows(get1[...], h_im)
    else:
        s_hre[...] = x_re.reshape(tl, S5_WIDTH)
        s_him[...] = x_im.reshape(tl, S5_WIDTH)

        def s5_body(j, carry):
            c_re, c_im = carry
            rs = pl.ds(pl.multiple_of(j * SUBLANES, SUBLANES), SUBLANES)
            hr = s_hre[rs, :] + p_re * c_re - p_im * c_im
            hi = s_him[rs, :] + p_re * c_im + p_im * c_re
            s_hre[rs, :] = hr
            s_him[rs, :] = hi
            return hr[SUBLANES - 1:SUBLANES, :], hi[SUBLANES - 1:SUBLANES, :]

        c_re, c_im = lax.fori_loop(0, n8, s5_body, (o_s5r[0], o_s5i[0]))
        o_s5r[0] = c_re
        o_s5i[0] = c_im
        h_re, h_im = s_hre[...], s_him[...]
    y5 = _dot_nt(h_re, s5_ct[0]) - _dot_nt(h_im, s5_ct[1]) + s5_d[...] * u
    yy = _dot(y5, glu_w[...])
    y5 = _rms(yy[:, 0:256] * _sigmoid(yy[:, 256:512]), s5_nrm[...])
    ymix[:, 256:512] = masked(y5)

    c = SCAN_CHUNK
    groups = range(tl // c)
    heads = range(4)
    pairs = [(i, h) for i in groups for h in heads]
    spg = c // SLOT
    r = lax.broadcasted_iota(jnp.int32, (c, c), 0)
    s = lax.broadcasted_iota(jnp.int32, (c, c), 1)
    if sample:
        same = (r >> 3) == (s >> 3)
        lower, strict, upper = (s <= r) & same, (s < r) & same, (r <= s) & same
    else:
        lower, strict, upper = s <= r, s < r, r <= s
    lower01, upper01 = lower.astype(BF16), upper.astype(BF16)
    grp_rows = [pl.ds(i * c, c) for i in groups]

    def expand(x):
        n = x.shape[1]
        rr = lax.broadcasted_iota(jnp.int32, (c, spg * n), 0) >> 3
        cc = lax.broadcasted_iota(jnp.int32, (c, spg * n), 1) >> (n.bit_length() - 1)
        return jnp.where(rr == cc, jnp.concatenate([x] * spg, axis=1), 0.0)

    la = [s_la[rs, :] for rs in grp_rows]
    cum = [_dot_exact01(lower01, x) for x in la]
    cum_t = [_dot_exact01_tn(x, upper01) for x in la]
    if sample:
        same01 = same.astype(BF16)
        cl = [_dot_exact01(same01, x) for x in la]
    else:
        cl = [x[c - 1:c, :] for x in cum]
    tail = [jnp.exp(a_ - b_) for a_, b_ in zip(cl, cum)]
    ecum = [jnp.exp(x) for x in cum]
    ecl = [jnp.exp(x) for x in cl]

    def col(x, ln):
        return x[:, ln:ln + 1]

    def dec_of(i, ln):
        return jnp.where(lower, jnp.exp(col(cum[i], ln) - cum_t[i][ln:ln + 1, :]), 0.0)

    xbc = [s_xbc[rs, :] for rs in grp_rows]
    dt = [s_dt[rs, :] for rs in grp_rows]
    bgs = [[xbc[i][:, 256 + g * SSD_STATE:256 + (g + 1) * SSD_STATE] for g in range(2)] for i in groups]
    cgs = [[xbc[i][:, 512 + g * SSD_STATE:512 + (g + 1) * SSD_STATE] for g in range(2)] for i in groups]
    cb = [[_dot_nt(cgs[i][g], bgs[i][g]) for g in range(2)] for i in groups]
    xh = {(i, h): xbc[i][:, h * SSD_HEADDIM:(h + 1) * SSD_HEADDIM] for i, h in pairs}
    xq = {p: xh[p] * col(dt[p[0]], SM_DT + p[1]) for p in pairs}
    y_ssd = {(i, h): _dot(cb[i][h // 2] * dec_of(i, SM_DT + h), xq[i, h]) + psm[2:3, h:h + 1] * xh[i, h]
             for i, h in pairs}
    bt = {(i, h): bgs[i][h // 2] * col(tail[i], SM_DT + h) for i, h in pairs}
    cq = {(i, h): cgs[i][h // 2] * col(ecum[i], SM_DT + h) for i, h in pairs}
    upd = {p: _dot_tn(expand(bt[p]) if sample else bt[p], xq[p]) for p in pairs}

    qkv = [s_qkv[rs, :] for rs in grp_rows]
    beta = [s_beta[rs, :] for rs in grp_rows]
    hd = GDN_HEADDIM
    gq_ = {(i, h): qkv[i][:, h * hd:(h + 1) * hd] for i, h in pairs}
    gk = {(i, h): qkv[i][:, 256 + h * hd:256 + (h + 1) * hd] for i, h in pairs}
    gv = {(i, h): qkv[i][:, 512 + h * hd:512 + (h + 1) * hd] for i, h in pairs}
    gdec = {(i, h): dec_of(i, SM_GA + h) for i, h in pairs}
    gb = {(i, h): col(beta[i], SM_GB + h) for i, h in pairs}
    kk = {p: _dot_nt(gk[p], gk[p]) for p in pairs}
    qk_d = {p: _dot_nt(gq_[p], gk[p]) * gdec[p] for p in pairs}
    a_mat = {p: jnp.where(strict, gdec[p] * kk[p], 0.0) * gb[p] for p in pairs}
    t_inv = dict(zip(pairs, _unit_lower_inverse([a_mat[p] for p in pairs], c,
                                                blk=SLOT if sample else 2 * SUBLANES, merge=not sample)))
    ge = {(i, h): col(ecum[i], SM_GA + h) for i, h in pairs}
    rhs = {p: jnp.concatenate([gv[p] * gb[p], gk[p] * (gb[p] * ge[p])], axis=1) for p in pairs}
    x = {p: _dot(t_inv[p], rhs[p]) for p in pairs}
    ktl = {(i, h): gk[i, h] * col(tail[i], SM_GA + h) for i, h in pairs}
    gqe = {p: gq_[p] * ge[p] for p in pairs}

    if sample:
        slots = [pl.ds(i * spg, spg) for i in groups]
        last = [ecl[i].reshape(spg, SLOT, 128)[:, SLOT - 1:SLOT, :] for i in groups]
        s_all = {(i, h): st_gdn[slots[i], h] for i, h in pairs}
        m = {p: jnp.concatenate([expand(x[p][:, hd:2 * hd]), expand(gqe[p])], axis=0) for p in pairs}
        rr = {p: _dot(m[p], s_all[p].reshape(spg * hd, hd)) for p in pairs}
        delta = {p: x[p][:, 0:hd] - rr[p][0:c] for p in pairs}
        og = {p: rr[p][c:2 * c] + _dot(qk_d[p], delta[p]) for p in pairs}
        un = {p: _dot_tn(expand(ktl[p]), delta[p]) for p in pairs}
        h_all = {(i, h): st_ssd[slots[i], h] for i, h in pairs}
        yi = {p: _dot(expand(cq[p]), h_all[p].reshape(spg * SSD_STATE, SSD_HEADDIM)) for p in pairs}
        for i, h in pairs:
            rs = grp_rows[i]
            o_gdn[slots[i], h] = (s_all[i, h] * last[i][:, :, SM_GA + h:SM_GA + h + 1]
                                  + un[i, h].reshape(spg, hd, hd))
            s_y[rs, 512 + h * hd:512 + (h + 1) * hd] = og[i, h]
            o_ssd[slots[i], h] = (h_all[i, h] * last[i][:, :, SM_DT + h:SM_DT + h + 1]
                                  + upd[i, h].reshape(spg, SSD_STATE, SSD_HEADDIM))
            s_y[rs, h * SSD_HEADDIM:(h + 1) * SSD_HEADDIM] = y_ssd[i, h] + yi[i, h]
    else:
        fg = {p: _dot_tn(ktl[p], x[p]) for p in pairs}
        qo = {p: _dot(qk_d[p], x[p]) for p in pairs}
        for i in groups:
            rs = grp_rows[i]
            s0 = [o_gdn[0, h] for h in heads]
            h0 = [o_ssd[0, h] for h in heads]
            m = [jnp.concatenate([gqe[i, h] - qo[i, h][:, hd:2 * hd], fg[i, h][:, hd:2 * hd]], axis=0) for h in heads]
            rr = [_dot(m[h], s0[h]) for h in heads]
            yi = [_dot(cq[i, h], h0[h]) for h in heads]
            for h in heads:
                o_gdn[0, h] = s0[h] * col(ecl[i], SM_GA + h) - rr[h][c:c + hd] + fg[i, h][:, 0:hd]
                s_y[rs, 512 + h * hd:512 + (h + 1) * hd] = rr[h][0:c] + qo[i, h][:, 0:hd]
                o_ssd[0, h] = h0[h] * col(ecl[i], SM_DT + h) + upd[i, h]
                s_y[rs, h * SSD_HEADDIM:(h + 1) * SSD_HEADDIM] = y_ssd[i, h] + yi[h]

    cs = SLOT if sample else GLA_CHUNK
    n_sub, sh = tl // cs, cs.bit_length() - 1
    rt = lax.broadcasted_iota(jnp.int32, (tl, tl), 0)
    ct = lax.broadcasted_iota(jnp.int32, (tl, tl), 1)
    sub_same = (rt >> sh) == (ct >> sh)
    m_cum = sub_same & (ct <= rt)
    m_ref = sub_same & ((ct & (cs - 1)) < cs // 2)
    m3 = jnp.concatenate([m_cum.astype(BF16), m_ref.astype(BF16), sub_same.astype(BF16)], axis=0)
    lg = s_lg[...]
    c3 = _dot_exact01(m3, lg)
    cumg, refg, clg = c3[0:tl], c3[tl:2 * tl], c3[2 * tl:3 * tl]
    gla_q = zin[:, Z_Q:Z_Q + 128] * (GLA_DK ** -0.5)
    gla_k = masked(zin[:, Z_K:Z_K + 128])
    gla_v = zin[:, Z_V:Z_V + 256]
    qe, ke = gla_q * jnp.exp(cumg - refg), gla_k * jnp.exp(refg - cumg)
    qd, kt = gla_q * jnp.exp(cumg), gla_k * jnp.exp(clg - cumg)
    klane = lax.broadcasted_iota(jnp.int32, (1, 128), 1) >> 5
    vlane = lax.broadcasted_iota(jnp.int32, (1, 256), 1) >> 6
    sc = [jnp.where(m_cum, _dot_nt(jnp.where(klane == h, qe, 0.0), ke), 0.0) for h in heads]
    of = [_dot(sc[h], gla_v) for h in heads]
    o_gl = jnp.where(vlane == 0, of[0], 0.0)
    for h in range(1, GLA_HEADS):
        o_gl = o_gl + jnp.where(vlane == h, of[h], 0.0)

    wide = n_sub * 128

    def expand_sub(x):
        rr_ = lax.broadcasted_iota(jnp.int32, (tl, wide), 0) >> sh
        cc_ = lax.broadcasted_iota(jnp.int32, (tl, wide), 1) >> 7
        return jnp.where(rr_ == cc_, jnp.concatenate([x] * n_sub, axis=1), 0.0)

    head_diag = (((lax.broadcasted_iota(jnp.int32, (wide, 256), 0) >> 5) & 3)
                 == (lax.broadcasted_iota(jnp.int32, (wide, 256), 1) >> 6))
    u_all = jnp.where(head_diag, _dot_tn(expand_sub(kt), gla_v), 0.0)
    msub = ((lax.broadcasted_iota(jnp.int32, (tl, n_sub), 0) >> sh)
            == lax.broadcasted_iota(jnp.int32, (tl, n_sub), 1)).astype(BF16)
    dcol = jnp.exp(_dot_exact01_tn(lg, msub))
    if sample:
        w2 = st_gla[...].reshape(n_sub * 128, GLA_DV)
        e4 = ((lax.broadcasted_iota(jnp.int32, (GLA_DV, 256), 1) & (GLA_DV - 1))
              == lax.broadcasted_iota(jnp.int32, (GLA_DV, 256), 0)).astype(BF16)
        w_all = jnp.where(head_diag, _dot_x_exact01(w2, e4), 0.0)
    else:
        ws = []
        w = w_scr[...]
        for i in range(n_sub):
            ws.append(w)
            w = w * dcol[:, i:i + 1] + u_all[i * 128:(i + 1) * 128, :]
        w_scr[...] = w
        w_all = jnp.concatenate(ws, axis=0)
        for h in heads:
            o_gla[0, h] = w[h * GLA_DK:(h + 1) * GLA_DK, h * GLA_DV:(h + 1) * GLA_DV]
    s_y[:, 768:1024] = o_gl + _dot(expand_sub(qd), w_all)
    if sample:
        for i in range(n_sub):
            w = w_all[i * 128:(i + 1) * 128, :] * dcol[:, i:i + 1] + u_all[i * 128:(i + 1) * 128, :]
            for h in heads:
                o_gla[i, h] = w[h * GLA_DK:(h + 1) * GLA_DK, h * GLA_DV:(h + 1) * GLA_DV]

    e256 = e01[0:256, 0:256]
    y = s_y[:, 0:256] * _silu(zin[:, Z_Z:Z_Z + 256])
    ymix[:, 0:256] = masked(_rms(y, ssd_nrm[...]))
    o = s_y[:, 512:768]
    o = o * lax.rsqrt(_head_meansq(o, e256, GDN_HEADDIM) + EPS) * gdn_nrm[...]
    ymix[:, 512:768] = masked(o * _silu(zin[:, Z_GZ:Z_GZ + 256]))
    o = s_y[:, 768:1024]
    o = o * lax.rsqrt(_head_meansq(o, e256, GLA_DV) + EPS) * gla_nrm[...]
    ymix[:, 768:1024] = masked(o * _silu(zin[:, Z_GG:Z_GG + 256]))


def _slot_move_matrices(tl, rows):
    slots = tl // SLOT
    put = np.zeros((tl, rows * slots), np.float32)
    get = np.zeros((rows * slots, tl), np.float32)
    for g in range(slots):
        for j in range(rows):
            put[g * SLOT + SLOT_OFF - rows + j, g * rows + j] = 1.0
            get[g * rows + j, g * SLOT + SLOT - rows + j] = 1.0
    return jnp.asarray(put, BF16), jnp.asarray(get, BF16)


def _mixers(zin, l, prm, *, batch, rows_per_seq, sample, states=None):
    t = zin.shape[0]
    tl = min(SAMPLE_MIX_TILE if sample else TILE_TOKENS, t)
    if sample:
        slots = tl // SLOT
        grid = (t // tl,)
        tok = lambda w: pl.BlockSpec((tl, w), lambda i: (i, 0))
        st4 = lambda a, b_, c_: pl.BlockSpec((slots, a, b_, c_), lambda i: (i, 0, 0, 0))
        st_in = lambda a, b_, c_: pl.BlockSpec((None, slots, a, b_, c_), lambda i: (l, i, 0, 0, 0))
        rows_in = lambda n, w: pl.BlockSpec((None, n * slots, w), lambda i: (l, i, 0))
        rows_out = lambda n, w: pl.BlockSpec((n * slots, w), lambda i: (i, 0))
    else:
        tiles = rows_per_seq // tl
        grid = (batch, tiles)
        tok = lambda w: pl.BlockSpec((tl, w), lambda b, j: (b * tiles + j, 0))
        st4 = lambda a, b_, c_: pl.BlockSpec((1, a, b_, c_), lambda b, j: (b, 0, 0, 0))
    names = ["psmall", "ssd_cw", "ssd_cb", "ssd_nrm", "s5_b", "s5_ct", "s5_l1", "s5_ps" if sample else "s5_pp",
             "s5_d", "glu_w", "s5_nrm", "gdn_cw", "gdn_nrm", "gla_w2", "gla_bg", "gla_nrm"]
    consts = [prm[n] for n in names]
    in_specs = [tok(NZ)] + [_layer_spec(a, l, len(grid)) for a in consts] + [_shared_spec(prm["e01"])]
    operands = [zin] + consts + [prm["e01"]]
    nseq = t // SLOT if sample else batch
    if sample:
        put3, get3 = _slot_move_matrices(tl, CONV_W - 1)
        _, get1 = _slot_move_matrices(tl, 1)
        in_specs += [st_in(SSD_HEADS, SSD_STATE, SSD_HEADDIM), rows_in(CONV_W - 1, SSD_CONV_DIM),
                     rows_in(1, S5_WIDTH), rows_in(1, S5_WIDTH),
                     st_in(GDN_HEADS, GDN_HEADDIM, GDN_HEADDIM), rows_in(CONV_W - 1, GDN_CONV_DIM),
                     st_in(GLA_HEADS, GLA_DK, GLA_DV), _shared_spec(put3), _shared_spec(get3), _shared_spec(get1)]
        operands += list(states) + [put3, get3, get1]
        cv_spec = rows_out(CONV_W - 1, 768)
        cv_shape = jax.ShapeDtypeStruct((nseq * (CONV_W - 1), 768), F32)
        s5_spec = rows_out(1, S5_WIDTH)
        s5_shape = jax.ShapeDtypeStruct((nseq, S5_WIDTH), F32)
    else:
        cv_spec = pl.BlockSpec((1, CONV_W - 1, 768), lambda b, j: (b, 0, 0))
        cv_shape = jax.ShapeDtypeStruct((batch, CONV_W - 1, 768), F32)
        s5_spec = pl.BlockSpec((1, 1, S5_WIDTH), lambda b, j: (b, 0, 0))
        s5_shape = jax.ShapeDtypeStruct((batch, 1, S5_WIDTH), F32)
    out_specs = [tok(D_MODEL), st4(SSD_HEADS, SSD_STATE, SSD_HEADDIM), cv_spec, s5_spec, s5_spec,
                 st4(GDN_HEADS, GDN_HEADDIM, GDN_HEADDIM), cv_spec, st4(GLA_HEADS, GLA_DK, GLA_DV)]
    out_shape = [jax.ShapeDtypeStruct((t, D_MODEL), F32),
                 jax.ShapeDtypeStruct((nseq, SSD_HEADS, SSD_STATE, SSD_HEADDIM), F32), cv_shape, s5_shape, s5_shape,
                 jax.ShapeDtypeStruct((nseq, GDN_HEADS, GDN_HEADDIM, GDN_HEADDIM), F32), cv_shape,
                 jax.ShapeDtypeStruct((nseq, GLA_HEADS, GLA_DK, GLA_DV), F32)]
    scratch = [pltpu.VMEM((tl + SUBLANES, 768), F32), pltpu.VMEM((tl + SUBLANES, 768), F32),
               pltpu.VMEM((tl, 768), F32), pltpu.VMEM((tl, 128), F32), pltpu.VMEM((tl, 128), F32),
               pltpu.VMEM((tl, 128), F32), pltpu.VMEM((tl, 768), F32), pltpu.VMEM((tl, 128), F32),
               pltpu.VMEM((tl, D_MODEL), F32), pltpu.VMEM((tl, S5_WIDTH), F32), pltpu.VMEM((tl, S5_WIDTH), F32),
               pltpu.VMEM((GLA_HEADS * GLA_DK, GLA_HEADS * GLA_DV), F32)]
    return pl.pallas_call(
        functools.partial(_mixers_kernel, tl=tl, sample=sample),
        grid=grid, in_specs=in_specs, out_specs=out_specs, out_shape=out_shape, scratch_shapes=scratch,
        compiler_params=pltpu.CompilerParams(dimension_semantics=("arbitrary",) * len(grid),
                                             vmem_limit_bytes=VMEM_LIMIT),
        name="mixers_sample" if sample else "mixers_prompt",
    )(*operands)


def _outffn_kernel(*refs, tm, sample, final):
    x_ref, y_ref, g1_ref, sh_ref, sc_ref, g2_ref, nf_ref, wout, up, cw, cb, down = refs[:12]
    k = 12
    hist = put2 = get2 = fin = None
    if sample:
        hist, put2, get2 = refs[k:k + 3]
        k += 3
    if final:
        fin = refs[k]
        k += 1
    x2_ref, cv_ref, ext = refs[k], refs[k + 1], refs[k + 2]

    if sample:
        ex = lambda r: _expand_rows(r[...], SLOT)
        g1, sh, sc, g2 = ex(g1_ref), ex(sh_ref), ex(sc_ref), ex(g2_ref)
        valid = (lax.broadcasted_iota(jnp.int32, (tm, 1), 0) & (SLOT - 1)) >= SLOT_OFF
        ext[0:SUBLANES, :] = jnp.zeros((SUBLANES, 2 * D_FF), F32)
    else:
        g1, sh, sc, g2 = g1_ref[0], sh_ref[0], sc_ref[0], g2_ref[0]

        @pl.when(pl.program_id(1) == 0)
        def _():
            ext[0:SUBLANES, :] = jnp.zeros((SUBLANES, 2 * D_FF), F32)

    x1 = x_ref[...] + g1 * _dot(y_ref[...], wout[...])
    h = (_rms(x1, nf_ref[...]) * (1.0 + sc) + sh).astype(BF16)
    for j in range(0, 2 * D_FF, FF_CHUNK):
        uj = jnp.dot(h, up[:, j:j + FF_CHUNK], preferred_element_type=F32)
        if sample:
            uj = jnp.where(valid, uj, _place_rows(put2[...], hist[:, j:j + FF_CHUNK]))
        ext[pl.ds(SUBLANES, tm), j:j + FF_CHUNK] = uj

    def conv(col):
        acc = cb[:, col:col + FF_CHUNK]
        for kk in range(FFN_CONV_W):
            acc = acc + cw[kk:kk + 1, col:col + FF_CHUNK] * ext[pl.ds(SUBLANES - (FFN_CONV_W - 1) + kk, tm),
                                                                 col:col + FF_CHUNK]
        return acc

    f = jnp.zeros((tm, D_MODEL), F32)
    for j in range(0, D_FF, FF_CHUNK):
        act = _silu(conv(j)) * conv(D_FF + j)
        f = f + _dot(act, down[j:j + FF_CHUNK, :])
    x2 = x1 + g2 * f
    if final:
        x2 = _rms(x2, fin[...])
    if sample:
        x2_ref[...] = jnp.where(valid, x2, 0.0)
        for j in range(0, 2 * D_FF, FF_CHUNK):
            cv_ref[:, j:j + FF_CHUNK] = _place_rows(get2[...], ext[pl.ds(SUBLANES, tm), j:j + FF_CHUNK])
    else:
        x2_ref[...] = x2
        tail = ext[pl.ds(tm + SUBLANES - (FFN_CONV_W - 1), FFN_CONV_W - 1), :]
        cv_ref[0] = tail
        ext[pl.ds(SUBLANES - (FFN_CONV_W - 1), FFN_CONV_W - 1), :] = tail


def _out_ffn(x2d, ymix, mod, l, prm, *, batch, rows_per_seq, sample, hist=None, final_gain=None):
    t = x2d.shape[0]
    tm = min(SAMPLE_FFN_TILE if sample else TILE_TOKENS, t)
    final = final_gain is not None
    nrows = FFN_CONV_W - 1
    if sample:
        slots = tm // SLOT
        grid = (t // tm,)
        tok = lambda w: pl.BlockSpec((tm, w), lambda i: (i, 0))
        mspecs = [pl.BlockSpec((None, slots, D_MODEL), (lambda i, j=j: (l, i, j))) for j in (2, 3, 4, 5)]
        mops = [mod] * 4
        cv_spec = pl.BlockSpec((nrows * slots, 2 * D_FF), lambda i: (i, 0))
        cv_shape = jax.ShapeDtypeStruct((t // SLOT * nrows, 2 * D_FF), F32)
    else:
        tiles = rows_per_seq // tm
        grid = (batch, tiles)
        tok = lambda w: pl.BlockSpec((tm, w), lambda b, j: (b * tiles + j, 0))
        m3 = mod.reshape(DEPTH * batch * 6, 1, D_MODEL)
        mspecs = [pl.BlockSpec((1, 1, D_MODEL), (lambda b, i, j=j: ((l * batch + b) * 6 + j, 0, 0))) for j in (2, 3, 4, 5)]
        mops = [m3] * 4
        cv_spec = pl.BlockSpec((1, nrows, 2 * D_FF), lambda b, j: (b, 0, 0))
        cv_shape = jax.ShapeDtypeStruct((batch, nrows, 2 * D_FF), F32)
    consts = [prm[n] for n in ("norm_ffn", "w_out", "ffn_up", "ffn_cw", "ffn_cb", "ffn_down")]
    in_specs = [tok(D_MODEL), tok(D_MODEL)] + mspecs + [_layer_spec(a, l, len(grid)) for a in consts]
    operands = [x2d, ymix] + mops + consts
    if sample:
        put2, get2 = _slot_move_matrices(tm, nrows)
        in_specs += [pl.BlockSpec((None, nrows * slots, 2 * D_FF), lambda i: (l, i, 0)),
                     _shared_spec(put2), _shared_spec(get2)]
        operands += [hist, put2, get2]
    if final:
        in_specs.append(_shared_spec(final_gain))
        operands.append(final_gain)
    return pl.pallas_call(
        functools.partial(_outffn_kernel, tm=tm, sample=sample, final=final),
        grid=grid, in_specs=in_specs,
        out_specs=[tok(D_MODEL), cv_spec],
        out_shape=[jax.ShapeDtypeStruct((t, D_MODEL), F32), cv_shape],
        scratch_shapes=[pltpu.VMEM((tm + SUBLANES, 2 * D_FF), F32)],
        compiler_params=pltpu.CompilerParams(dimension_semantics=("arbitrary",) * len(grid),
                                             vmem_limit_bytes=VMEM_LIMIT),
        name="out_ffn_sample" if sample else "out_ffn_prompt",
    )(*operands)


def _reorder_w_in(w_in):
    o = np.cumsum([0, 256, 768, 4, 256, 768, 256, 4, 4, 128, 128, 256, 256, 16])
    seg = lambda i: w_in[:, :, o[i]:o[i + 1]]
    small = [seg(2), seg(6), seg(7), seg(12)]
    pad = jnp.zeros(w_in.shape[:2] + (128 - 28,), w_in.dtype)
    cols = [seg(1), seg(4), seg(0), seg(3), seg(5), seg(10), seg(11), seg(8), seg(9)] + small + [pad]
    return jnp.concatenate(cols, axis=-1).astype(BF16)


def _pad_lanes(a, lane0, width=128):
    return jnp.pad(a, ((0, 0), (lane0, width - lane0 - a.shape[1])))[:, None, :]


def kernel(x_prompt, x_sample, c_prompt, c_sample, state_ssd, state_ssd_conv, state_s5_re, state_s5_im,
           state_gdn, state_gdn_conv, state_gla, state_ffn_conv, ada_w, ada_b, norm_mix, norm_ffn, w_in, w_out,
           ssd_conv_w, ssd_conv_b, ssd_dt_bias, ssd_a_log, ssd_d, ssd_norm, s5_a_re, s5_a_im, s5_log_dt,
           s5_b_re, s5_b_im, s5_c_re, s5_c_im, s5_d, s5_glu_w, s5_norm, gdn_conv_w, gdn_a_log, gdn_dt_bias,
           gdn_norm, gla_wg2, gla_bg, gla_norm, ffn_up, ffn_conv_w, ffn_conv_b, ffn_down, final_norm):
    nb, seq = x_prompt.shape[0], x_prompt.shape[1]
    ns, dseq = x_sample.shape[0], x_sample.shape[1]
    assert dseq == SLOT - SLOT_OFF

    mod_p, mod_s = _modulation(c_prompt.astype(F32), c_sample.astype(F32), ada_w, ada_b)
    s5_b, s5_ct, s5_l1, s5_pp, s5_ps = _s5_prepare(s5_a_re, s5_a_im, s5_log_dt, s5_b_re, s5_b_im, s5_c_re, s5_c_im)

    row = lambda a: a[:, None, :]
    psmall = jnp.concatenate([
        _pad_lanes(ssd_dt_bias, SM_DT) + _pad_lanes(gdn_dt_bias, SM_GA),
        _pad_lanes(ssd_a_log, SM_DT) + _pad_lanes(gdn_a_log, SM_GA),
        _pad_lanes(ssd_d, 0), jnp.zeros((DEPTH, SUBLANES - 3, 128), F32)], axis=1)
    prm = dict(
        psmall=psmall, ssd_cw=ssd_conv_w, ssd_cb=row(ssd_conv_b), ssd_nrm=row(ssd_norm),
        s5_b=s5_b, s5_ct=s5_ct, s5_l1=s5_l1, s5_pp=s5_pp, s5_ps=s5_ps, s5_d=row(s5_d),
        glu_w=s5_glu_w.astype(BF16), s5_nrm=row(s5_norm), gdn_cw=gdn_conv_w,
        gdn_nrm=row(jnp.tile(gdn_norm, (1, GDN_HEADS))),
        gla_w2=jnp.pad(gla_wg2, ((0, 0), (SM_LR, 128 - SM_LR - GLA_GATE_RANK), (0, 0))).astype(BF16),
        gla_bg=row(gla_bg), gla_nrm=row(jnp.tile(gla_norm, (1, GLA_HEADS))),
        e01=jnp.asarray(np.arange(512)[:, None] // 64 == np.arange(512)[None, :] // 64, BF16),
        norm_mix=row(norm_mix), norm_ffn=row(norm_ffn), w_in=_reorder_w_in(w_in), w_out=w_out.astype(BF16),
        ffn_up=ffn_up.astype(BF16), ffn_cw=ffn_conv_w, ffn_cb=row(ffn_conv_b), ffn_down=ffn_down.astype(BF16))
    fin = final_norm.reshape(1, D_MODEL)

    xp = x_prompt.astype(F32).reshape(nb * seq, D_MODEL)
    p_new = []
    for l in range(DEPTH):
        zin = _input_projection(xp, mod_p, l, prm, rows_per_seq=seq, sample=False)
        ymix, h_ssd, cv_ssd, h5r, h5i, s_gdn, cv_gdn, s_gla = _mixers(zin, l, prm, batch=nb, rows_per_seq=seq,
                                                                      sample=False)
        xp, cv_ffn = _out_ffn(xp, ymix, mod_p, l, prm, batch=nb, rows_per_seq=seq, sample=False,
                              final_gain=fin if l == DEPTH - 1 else None)
        p_new.append((h_ssd, cv_ssd, h5r.reshape(nb, S5_GROUPS, S5_STATE), h5i.reshape(nb, S5_GROUPS, S5_STATE),
                      s_gdn, cv_gdn, s_gla, cv_ffn))

    xs = jnp.pad(x_sample.astype(F32), ((0, 0), (SLOT_OFF, 0), (0, 0))).reshape(ns * SLOT, D_MODEL)
    f32 = lambda a: a.astype(F32)
    states = (f32(state_ssd), f32(state_ssd_conv).reshape(DEPTH, ns * (CONV_W - 1), SSD_CONV_DIM),
              f32(state_s5_re).reshape(DEPTH, ns, S5_WIDTH), f32(state_s5_im).reshape(DEPTH, ns, S5_WIDTH),
              f32(state_gdn), f32(state_gdn_conv).reshape(DEPTH, ns * (CONV_W - 1), GDN_CONV_DIM), f32(state_gla))
    hist = f32(state_ffn_conv).reshape(DEPTH, ns * (FFN_CONV_W - 1), 2 * D_FF)
    s_new = []
    for l in range(DEPTH):
        zin = _input_projection(xs, mod_s, l, prm, rows_per_seq=SLOT, sample=True)
        ymix, h_ssd, cv_ssd, h5r, h5i, s_gdn, cv_gdn, s_gla = _mixers(zin, l, prm, batch=ns, rows_per_seq=SLOT,
                                                                      sample=True, states=states)
        xs, cv_ffn = _out_ffn(xs, ymix, mod_s, l, prm, batch=ns, rows_per_seq=SLOT, sample=True, hist=hist,
                              final_gain=fin if l == DEPTH - 1 else None)
        s_new.append((h_ssd, cv_ssd.reshape(ns, CONV_W - 1, SSD_CONV_DIM),
                      h5r.reshape(ns, S5_GROUPS, S5_STATE), h5i.reshape(ns, S5_GROUPS, S5_STATE),
                      s_gdn, cv_gdn.reshape(ns, CONV_W - 1, GDN_CONV_DIM), s_gla,
                      cv_ffn.reshape(ns, FFN_CONV_W - 1, 2 * D_FF)))

    y_p = xp.reshape(nb, seq, D_MODEL).astype(x_prompt.dtype)
    y_s = xs.reshape(ns, SLOT, D_MODEL)[:, SLOT_OFF:, :].astype(x_sample.dtype)
    p_st = [jnp.stack(t) for t in zip(*p_new)]
    s_st = [jnp.stack(t) for t in zip(*s_new)]
    return (y_p, y_s, *p_st, *s_st)
```

```python
import functools

import numpy as np
import jax
import jax.numpy as jnp
from jax import lax
from jax.experimental import pallas as pl
from jax.experimental.pallas import tpu as pltpu

F32 = jnp.float32
BF16 = jnp.bfloat16

D_MODEL = 1024
DEPTH = 2
GROUP_WIDTH = 256
CONV_W = 4
SSD_HEADS = 4
SSD_HEADDIM = 64
SSD_STATE = 128
SSD_CONV_DIM = 768
S5_GROUPS = 16
S5_GROUP_CH = 16
S5_STATE = 64
S5_WIDTH = S5_GROUPS * S5_STATE
GDN_HEADS = 4
GDN_HEADDIM = 64
GDN_CONV_DIM = 768
GLA_HEADS = 4
GLA_DK = 32
GLA_DV = 64
GLA_GATE_RANK = 16
GLA_TAU = 16.0
D_FF = 2816
FFN_CONV_W = 3
EPS = 1e-6

NZ = 3200
Z_XBC, Z_QKV, Z_Z, Z_U, Z_GZ, Z_V, Z_GG, Z_Q, Z_K, Z_SM = 0, 768, 1536, 1792, 2048, 2304, 2560, 2816, 2944, 3072
SM_DT, SM_GA, SM_GB, SM_LR = 0, 4, 8, 12

SUBLANES = 8
SLOT = SUBLANES
SLOT_OFF = SLOT - 4
SCAN_CHUNK = 64
GLA_CHUNK = 16
TILE_TOKENS = 256
SAMPLE_MIX_TILE = 128
SAMPLE_FFN_TILE = 128
FF_CHUNK = 256
VMEM_LIMIT = 56 * 1024 * 1024


def _silu(x):
    return x * (1.0 / (1.0 + jnp.exp(-x)))


def _sigmoid(x):
    return 1.0 / (1.0 + jnp.exp(-x))


def _softplus(x):
    return jnp.maximum(x, 0.0) + jnp.log(1.0 + jnp.exp(-jnp.abs(x)))


def _dot(a, b):
    return jnp.dot(a.astype(BF16), b.astype(BF16), preferred_element_type=F32)


def _dot_nt(a, b):
    return lax.dot_general(a.astype(BF16), b.astype(BF16), (((1,), (1,)), ((), ())), preferred_element_type=F32)


def _dot_tn(a, b):
    return lax.dot_general(a.astype(BF16), b.astype(BF16), (((0,), (0,)), ((), ())), preferred_element_type=F32)


def _split_hi_lo(x):
    hi = x.astype(BF16)
    lo = (x - hi.astype(F32)).astype(BF16)
    return hi, lo


def _dot_exact01(m01, x):
    hi, lo = _split_hi_lo(x)
    return jnp.dot(m01, hi, preferred_element_type=F32) + jnp.dot(m01, lo, preferred_element_type=F32)


def _dot_exact01_tn(x, m01):
    hi, lo = _split_hi_lo(x)
    dn = (((0,), (0,)), ((), ()))
    return (lax.dot_general(hi, m01, dn, preferred_element_type=F32)
            + lax.dot_general(lo, m01, dn, preferred_element_type=F32))


def _place_rows(p01, x):
    hi = x.astype(BF16)
    rest = x - hi.astype(F32)
    mid = rest.astype(BF16)
    lo = (rest - mid.astype(F32)).astype(BF16)
    mm = lambda v: jnp.dot(p01, v, preferred_element_type=F32)
    return (mm(hi) + mm(mid)) + mm(lo)


def _dot_x_exact01(x, m01):
    hi, lo = _split_hi_lo(x)
    return jnp.dot(hi, m01, preferred_element_type=F32) + jnp.dot(lo, m01, preferred_element_type=F32)


def _unit_lower_inverse(mats, c, blk, merge):
    r = lax.broadcasted_iota(jnp.int32, (c, c), 0)
    s = lax.broadcasted_iota(jnp.int32, (c, c), 1)
    sh = blk.bit_length() - 1
    diag = (r >> sh) == (s >> sh)
    eye = jnp.where(r == s, 1.0, 0.0)
    p = [jnp.where(diag, a, 0.0) for a in mats]
    t = [eye - d for d in p]
    pw = 2
    while pw < blk:
        p = [_dot(d, d) for d in p]
        t = [m + _dot(m, d) for m, d in zip(t, p)]
        pw *= 2
    size = blk
    while merge and size < c:
        sh = size.bit_length() - 1
        pair = ((r >> (sh + 1)) == (s >> (sh + 1))) & ((r >> sh) != (s >> sh))
        lt = [_dot(jnp.where(pair, a, 0.0), m) for a, m in zip(mats, t)]
        t = [m - _dot(m, y) for m, y in zip(t, lt)]
        size *= 2
    return t


def _rms(x, gain):
    return x * lax.rsqrt(jnp.mean(x * x, axis=-1, keepdims=True) + EPS) * gain


def _expand_rows(m, reps):
    g, n = m.shape
    return jnp.broadcast_to(m[:, None, :], (g, reps, n)).reshape(g * reps, n)


def _mod_kernel(cp_ref, cs_ref, w_ref, b_ref, op_ref, os_ref):
    w = w_ref[0].astype(BF16)
    op_ref[0] = _dot(_silu(cp_ref[...]), w) + b_ref[0]
    os_ref[0] = _dot(_silu(cs_ref[...]), w) + b_ref[0]


def _modulation(c_prompt, c_sample, ada_w, ada_b):
    nb, ns = c_prompt.shape[0], c_sample.shape[0]
    tn = 1024
    return pl.pallas_call(
        _mod_kernel,
        grid=(DEPTH, 6 * D_MODEL // tn),
        in_specs=[pl.BlockSpec((nb, D_MODEL), lambda l, j: (0, 0)),
                  pl.BlockSpec((ns, D_MODEL), lambda l, j: (0, 0)),
                  pl.BlockSpec((1, D_MODEL, tn), lambda l, j: (l, 0, j)),
                  pl.BlockSpec((1, 1, tn), lambda l, j: (l, 0, j))],
        out_specs=[pl.BlockSpec((1, nb, tn), lambda l, j: (l, 0, j)),
                   pl.BlockSpec((1, ns, tn), lambda l, j: (l, 0, j))],
        out_shape=[jax.ShapeDtypeStruct((DEPTH, nb, 6 * D_MODEL), F32),
                   jax.ShapeDtypeStruct((DEPTH, ns, 6 * D_MODEL), F32)],
        compiler_params=pltpu.CompilerParams(dimension_semantics=("arbitrary", "arbitrary"),
                                             vmem_limit_bytes=VMEM_LIMIT),
        name="adaln_mod",
    )(c_prompt, c_sample, ada_w, ada_b.reshape(DEPTH, 1, 6 * D_MODEL))


def _s5_prep_kernel(are_ref, aim_ref, ldt_ref, btr_ref, bti_ref, cr_ref, ci_ref, arow_ref, e_ref, gm_ref,
                    b_out, c_out, l1_out, pp_out, ps_out):
    a_re, a_im = are_ref[0], aim_ref[0]
    dt = jnp.exp(ldt_ref[0])
    mag = jnp.exp(dt * a_re)
    ab_re, ab_im = mag * jnp.cos(dt * a_im), mag * jnp.sin(dt * a_im)
    den = a_re * a_re + a_im * a_im
    f_re = ((ab_re - 1.0) * a_re + ab_im * a_im) / den
    f_im = (ab_im * a_re - (ab_re - 1.0) * a_im) / den
    bb_re = f_re * btr_ref[0] - f_im * bti_ref[0]
    bb_im = f_re * bti_ref[0] + f_im * btr_ref[0]
    e01, gm = e_ref[...], gm_ref[...]
    b_out[0, 0] = (_dot_x_exact01(bb_re, e01) * gm).astype(BF16)
    b_out[0, 1] = (_dot_x_exact01(bb_im, e01) * gm).astype(BF16)
    c_out[0, 0] = (_dot_x_exact01(cr_ref[0], e01) * gm).astype(BF16)
    c_out[0, 1] = (_dot_x_exact01(ci_ref[0], e01) * gm).astype(BF16)
    ar, ai, dtr = arow_ref[0, 0:1, :], arow_ref[0, 1:2, :], jnp.exp(arow_ref[0, 2:3, :])
    row = lax.broadcasted_iota(jnp.int32, (SUBLANES, 1), 0)

    def power(k):
        m = jnp.exp(k * dtr * ar)
        return m * jnp.cos(k * dtr * ai), m * jnp.sin(k * dtr * ai)

    for i, d in enumerate((1, 2, 4)):
        pr, pi = power(jnp.full((SUBLANES, 1), float(d), F32))
        keep = row >= d
        l1_out[0, i] = jnp.where(keep, pr, 0.0)
        l1_out[0, 3 + i] = jnp.where(keep, pi, 0.0)
    pr, pi = power((row + 1).astype(F32))
    pp_out[0, 0], pp_out[0, 1] = pr, pi
    pr, pi = power(jnp.maximum(row - (SLOT_OFF - 1), 0).astype(F32))
    ps_out[0, 0] = jnp.where(row >= SLOT_OFF, pr, 0.0)
    ps_out[0, 1] = jnp.where(row >= SLOT_OFF, pi, 0.0)


def _s5_prepare(s5_a_re, s5_a_im, s5_log_dt, s5_b_re, s5_b_im, s5_c_re, s5_c_im):
    rows = S5_GROUPS * S5_GROUP_CH
    rep = lambda a: jnp.repeat(a, S5_GROUP_CH, axis=1)
    ldt = jnp.broadcast_to(s5_log_dt[:, :, None], (DEPTH, S5_GROUPS, S5_STATE))
    bt = lambda b: jnp.swapaxes(b, 2, 3).reshape(DEPTH, rows, S5_STATE)
    arow = jnp.stack([s5_a_re.reshape(DEPTH, S5_WIDTH), s5_a_im.reshape(DEPTH, S5_WIDTH),
                      ldt.reshape(DEPTH, S5_WIDTH)], axis=1)
    arow = jnp.pad(arow, ((0, 0), (0, SUBLANES - 3), (0, 0)))
    e01 = np.zeros((S5_STATE, S5_WIDTH), np.float32)
    e01[np.arange(S5_WIDTH) % S5_STATE, np.arange(S5_WIDTH)] = 1.0
    gm = (np.arange(rows)[:, None] // S5_GROUP_CH == np.arange(S5_WIDTH)[None, :] // S5_STATE).astype(np.float32)
    p3 = lambda: pl.BlockSpec((1, rows, S5_STATE), lambda l: (l, 0, 0))
    tab = lambda n: pl.BlockSpec((1, n, SUBLANES, S5_WIDTH), lambda l: (l, 0, 0, 0))
    return pl.pallas_call(
        _s5_prep_kernel,
        grid=(DEPTH,),
        in_specs=[p3(), p3(), p3(), p3(), p3(), p3(), p3(),
                  pl.BlockSpec((1, SUBLANES, S5_WIDTH), lambda l: (l, 0, 0)),
                  pl.BlockSpec((S5_STATE, S5_WIDTH), lambda l: (0, 0)),
                  pl.BlockSpec((rows, S5_WIDTH), lambda l: (0, 0))],
        out_specs=[pl.BlockSpec((1, 2, rows, S5_WIDTH), lambda l: (l, 0, 0, 0)),
                   pl.BlockSpec((1, 2, rows, S5_WIDTH), lambda l: (l, 0, 0, 0)),
                   tab(6), tab(2), tab(2)],
        out_shape=[jax.ShapeDtypeStruct((DEPTH, 2, rows, S5_WIDTH), BF16),
                   jax.ShapeDtypeStruct((DEPTH, 2, rows, S5_WIDTH), BF16),
                   jax.ShapeDtypeStruct((DEPTH, 6, SUBLANES, S5_WIDTH), F32),
                   jax.ShapeDtypeStruct((DEPTH, 2, SUBLANES, S5_WIDTH), F32),
                   jax.ShapeDtypeStruct((DEPTH, 2, SUBLANES, S5_WIDTH), F32)],
        compiler_params=pltpu.CompilerParams(dimension_semantics=("arbitrary",), vmem_limit_bytes=VMEM_LIMIT),
        name="s5_prepare",
    )(rep(s5_a_re), rep(s5_a_im), rep(ldt), bt(s5_b_re), bt(s5_b_im),
      s5_c_re.reshape(DEPTH, rows, S5_STATE), s5_c_im.reshape(DEPTH, rows, S5_STATE),
      arow, jnp.asarray(e01, BF16), jnp.asarray(gm, F32))


def _inproj_kernel(x_ref, sh_ref, sc_ref, g_ref, w_ref, o_ref, *, sample):
    x = x_ref[...]
    hn = _rms(x, g_ref[...])
    if sample:
        sh, sc = _expand_rows(sh_ref[...], SLOT), _expand_rows(sc_ref[...], SLOT)
    else:
        sh, sc = sh_ref[0], sc_ref[0]
    h = (hn * (1.0 + sc) + sh).astype(BF16)
    nc = 640
    for j in range(0, NZ, nc):
        o_ref[:, j:j + nc] = jnp.dot(h, w_ref[:, j:j + nc], preferred_element_type=F32)


def _mod_specs(mod, l, chunk_ids, tm, tiles_per_seq, sample):
    if sample:
        return ([pl.BlockSpec((None, tm // SLOT, D_MODEL), (lambda i, j=j: (l, i, j))) for j in chunk_ids],
                [mod] * len(chunk_ids))
    nb = mod.shape[1]
    m3 = mod.reshape(DEPTH * nb * 6, 1, D_MODEL)
    return ([pl.BlockSpec((1, 1, D_MODEL), (lambda i, j=j: ((l * nb + i // tiles_per_seq) * 6 + j, 0, 0)))
             for j in chunk_ids], [m3] * len(chunk_ids))


def _layer_spec(a, l, grid_rank):
    nd = a.ndim - 1
    return pl.BlockSpec((None,) + a.shape[1:], lambda *_: (l,) + (0,) * nd, pipeline_mode=pl.Buffered(1))


def _shared_spec(a):
    nd = a.ndim
    return pl.BlockSpec(a.shape, lambda *_: (0,) * nd, pipeline_mode=pl.Buffered(1))


def _input_projection(x2d, mod, l, prm, *, rows_per_seq, sample):
    t = x2d.shape[0]
    tm = min(TILE_TOKENS, t)
    tiles_per_seq = max(rows_per_seq // tm, 1)
    mspecs, mops = _mod_specs(mod, l, (0, 1), tm, tiles_per_seq, sample)
    return pl.pallas_call(
        functools.partial(_inproj_kernel, sample=sample),
        grid=(t // tm,),
        in_specs=[pl.BlockSpec((tm, D_MODEL), lambda i: (i, 0))] + mspecs
                 + [_layer_spec(prm["norm_mix"], l, 1), _layer_spec(prm["w_in"], l, 1)],
        out_specs=pl.BlockSpec((tm, NZ), lambda i: (i, 0)),
        out_shape=jax.ShapeDtypeStruct((t, NZ), F32),
        compiler_params=pltpu.CompilerParams(dimension_semantics=("arbitrary",), vmem_limit_bytes=VMEM_LIMIT),
        name="in_projection",
    )(x2d, *mops, prm["norm_mix"], prm["w_in"])


def _causal_conv(ext_ref, xin, w_ref, width, tl):
    ext_ref[pl.ds(SUBLANES, tl), :] = xin
    acc = None
    for k in range(width):
        term = w_ref[k:k + 1, :] * ext_ref[pl.ds(SUBLANES - (width - 1) + k, tl), :]
        acc = term if acc is None else acc + term
    return acc


def _head_meansq(x, e01, width):
    return _dot_x_exact01(x * x, e01) * (1.0 / width)


def _mixers_kernel(*refs, tl, sample, n_carry):
    (zin, psm, ssd_cw, ssd_cb, ssd_nrm, s5_b, s5_ct, s5_l1, s5_p, s5_d, glu_w, s5_nrm,
     gdn_cw, gdn_nrm, gla_w2, gla_bg, gla_nrm, e01) = refs[:18]
    n_in = 18
    if sample:
        st_ssd, hist_ssd, st_s5r, st_s5i, st_gdn, hist_gdn, st_gla, put3, get3, get1 = refs[18:28]
        n_in = 28 + n_carry
    outs = refs[n_in:n_in + 8]
    ymix, o_ssd, o_ssd_cv, o_s5r, o_s5i, o_gdn, o_gdn_cv, o_gla = outs
    (ext_ssd, ext_gdn, s_xbc, s_dt, s_la, s_beta, s_qkv, s_lg, s_y, s_hre, s_him, w_scr) = refs[n_in + 8:]

    rows = lax.broadcasted_iota(jnp.int32, (tl, 1), 0)
    valid = (rows & (SLOT - 1)) >= SLOT_OFF if sample else None
    first = None if sample else (pl.program_id(1) == 0)

    def masked(x):
        return jnp.where(valid, x, 0.0) if sample else x

    if sample:
        ext_ssd[0:SUBLANES, :] = jnp.zeros((SUBLANES, SSD_CONV_DIM), F32)
        ext_gdn[0:SUBLANES, :] = jnp.zeros((SUBLANES, GDN_CONV_DIM), F32)
    else:
        @pl.when(first)
        def _():
            ext_ssd[0:SUBLANES, :] = jnp.zeros((SUBLANES, SSD_CONV_DIM), F32)
            ext_gdn[0:SUBLANES, :] = jnp.zeros((SUBLANES, GDN_CONV_DIM), F32)
            o_ssd[...] = jnp.zeros(o_ssd.shape, F32)
            o_gdn[...] = jnp.zeros(o_gdn.shape, F32)
            w_scr[...] = jnp.zeros(w_scr.shape, F32)
            o_s5r[...] = jnp.zeros(o_s5r.shape, F32)
            o_s5i[...] = jnp.zeros(o_s5i.shape, F32)

    def conv_in(ext_ref, col, hist_ref, w_ref, o_cv):
        xin = zin[:, col:col + 768]
        if sample:
            hist = hist_ref[...]
            xin = jnp.where(valid, xin, _place_rows(put3[...], hist.reshape(-1, hist.shape[-1])))
        y = _causal_conv(ext_ref, xin, w_ref, CONV_W, tl)
        if sample:
            o_cv[...] = _place_rows(get3[...], xin).reshape(o_cv.shape)
        else:
            tail = ext_ref[pl.ds(tl + SUBLANES - (CONV_W - 1), CONV_W - 1), :]
            o_cv[0] = tail
            ext_ref[pl.ds(SUBLANES - (CONV_W - 1), CONV_W - 1), :] = tail
        return y

    s_xbc[...] = _silu(conv_in(ext_ssd, Z_XBC, hist_ssd if sample else None, ssd_cw, o_ssd_cv) + ssd_cb[...])

    sm = zin[:, Z_SM:Z_SM + 128]
    sp = _softplus(sm + psm[0:1, :])
    s_dt[...] = masked(sp)
    s_la[...] = masked(-jnp.exp(psm[1:2, :]) * sp)
    s_beta[...] = masked(_sigmoid(sm))
    s_lg[...] = masked(-_softplus(-(_dot(sm, gla_w2[...]) + gla_bg[...])) * (1.0 / GLA_TAU))

    qkv = _silu(conv_in(ext_gdn, Z_QKV, hist_gdn if sample else None, gdn_cw, o_gdn_cv))
    qk = qkv[:, 0:512]
    inv = lax.rsqrt(_dot_x_exact01(qk * qk, e01[...]) + EPS)
    s_qkv[:, 0:256] = qk[:, 0:256] * inv[:, 0:256] * (GDN_HEADDIM ** -0.5)
    s_qkv[:, 256:512] = qk[:, 256:512] * inv[:, 256:512]
    s_qkv[:, 512:768] = qkv[:, 512:768]

    u = zin[:, Z_U:Z_U + 256]
    n8 = tl // SUBLANES
    x_re = masked(_dot(u, s5_b[0])).reshape(n8, SUBLANES, S5_WIDTH)
    x_im = masked(_dot(u, s5_b[1])).reshape(n8, SUBLANES, S5_WIDTH)
    for i, d in enumerate((1, 2, 4)):
        sr, si = pltpu.roll(x_re, d, 1), pltpu.roll(x_im, d, 1)
        ar, ai = s5_l1[i][None], s5_l1[3 + i][None]
        x_re, x_im = x_re + ar * sr - ai * si, x_im + ar * si + ai * sr
    p_re, p_im = s5_p[0], s5_p[1]
    if sample:
        c_re, c_im = st_s5r[...][:, None, :], st_s5i[...][:, None, :]
        h_re = (x_re + p_re[None] * c_re - p_im[None] * c_im).reshape(tl, S5_WIDTH)
        h_im = (x_im + p_re[None] * c_im + p_im[None] * c_re).reshape(tl, S5_WIDTH)
        o_s5r[...] = _place_rows(get1[...], h_re)
        o_s5i[...] = _place_rows(get1[...], h_im)
    else:
        s_hre[...] = x_re.reshape(tl, S5_WIDTH)
        s_him[...] = x_im.reshape(tl, S5_WIDTH)

        def s5_body(j, carry):
            c_re, c_im = carry
            rs = pl.ds(pl.multiple_of(j * SUBLANES, SUBLANES), SUBLANES)
            hr = s_hre[rs, :] + p_re * c_re - p_im * c_im
            hi = s_him[rs, :] + p_re * c_im + p_im * c_re
            s_hre[rs, :] = hr
            s_him[rs, :] = hi
            return hr[SUBLANES - 1:SUBLANES, :], hi[SUBLANES - 1:SUBLANES, :]

        c_re, c_im = lax.fori_loop(0, n8, s5_body, (o_s5r[0], o_s5i[0]))
        o_s5r[0] = c_re
        o_s5i[0] = c_im
        h_re, h_im = s_hre[...], s_him[...]
    y5 = _dot_nt(h_re, s5_ct[0]) - _dot_nt(h_im, s5_ct[1]) + s5_d[...] * u
    yy = _dot(y5, glu_w[...])
    y5 = _rms(yy[:, 0:256] * _sigmoid(yy[:, 256:512]), s5_nrm[...])
    ymix[:, 256:512] = masked(y5)

    c = SCAN_CHUNK
    groups = range(tl // c)
    heads = range(4)
    pairs = [(i, h) for i in groups for h in heads]
    spg = c // SLOT
    r = lax.broadcasted_iota(jnp.int32, (c, c), 0)
    s = lax.broadcasted_iota(jnp.int32, (c, c), 1)
    if sample:
        same = (r >> 3) == (s >> 3)
        lower, strict, upper = (s <= r) & same, (s < r) & same, (r <= s) & same
    else:
        lower, strict, upper = s <= r, s < r, r <= s
    lower01, upper01 = lower.astype(BF16), upper.astype(BF16)
    grp_rows = [pl.ds(i * c, c) for i in groups]

    def expand(x):
        n = x.shape[1]
        rr = lax.broadcasted_iota(jnp.int32, (c, spg * n), 0) >> 3
        cc = lax.broadcasted_iota(jnp.int32, (c, spg * n), 1) >> (n.bit_length() - 1)
        return jnp.where(rr == cc, jnp.concatenate([x] * spg, axis=1), 0.0)

    la = [s_la[rs, :] for rs in grp_rows]
    cum = [_dot_exact01(lower01, x) for x in la]
    cum_t = [_dot_exact01_tn(x, upper01) for x in la]
    if sample:
        same01 = same.astype(BF16)
        cl = [_dot_exact01(same01, x) for x in la]
    else:
        cl = [x[c - 1:c, :] for x in cum]
    tail = [jnp.exp(a_ - b_) for a_, b_ in zip(cl, cum)]
    ecum = [jnp.exp(x) for x in cum]
    ecl = [jnp.exp(x) for x in cl]

    def col(x, ln):
        return x[:, ln:ln + 1]

    def dec_of(i, ln):
        return jnp.where(lower, jnp.exp(col(cum[i], ln) - cum_t[i][ln:ln + 1, :]), 0.0)

    xbc = [s_xbc[rs, :] for rs in grp_rows]
    dt = [s_dt[rs, :] for rs in grp_rows]
    bgs = [[xbc[i][:, 256 + g * SSD_STATE:256 + (g + 1) * SSD_STATE] for g in range(2)] for i in groups]
    cgs = [[xbc[i][:, 512 + g * SSD_STATE:512 + (g + 1) * SSD_STATE] for g in range(2)] for i in groups]
    cb = [[_dot_nt(cgs[i][g], bgs[i][g]) for g in range(2)] for i in groups]
    xh = {(i, h): xbc[i][:, h * SSD_HEADDIM:(h + 1) * SSD_HEADDIM] for i, h in pairs}
    xq = {p: xh[p] * col(dt[p[0]], SM_DT + p[1]) for p in pairs}
    y_ssd = {(i, h): _dot(cb[i][h // 2] * dec_of(i, SM_DT + h), xq[i, h]) + psm[2:3, h:h + 1] * xh[i, h]
             for i, h in pairs}
    bt = {(i, h): bgs[i][h // 2] * col(tail[i], SM_DT + h) for i, h in pairs}
    cq = {(i, h): cgs[i][h // 2] * col(ecum[i], SM_DT + h) for i, h in pairs}
    upd = {p: _dot_tn(xq[p], expand(bt[p]) if sample else bt[p]) for p in pairs}

    qkv = [s_qkv[rs, :] for rs in grp_rows]
    beta = [s_beta[rs, :] for rs in grp_rows]
    hd = GDN_HEADDIM
    gq_ = {(i, h): qkv[i][:, h * hd:(h + 1) * hd] for i, h in pairs}
    gk = {(i, h): qkv[i][:, 256 + h * hd:256 + (h + 1) * hd] for i, h in pairs}
    gv = {(i, h): qkv[i][:, 512 + h * hd:512 + (h + 1) * hd] for i, h in pairs}
    gdec = {(i, h): dec_of(i, SM_GA + h) for i, h in pairs}
    gb = {(i, h): col(beta[i], SM_GB + h) for i, h in pairs}
    kk = {p: _dot_nt(gk[p], gk[p]) for p in pairs}
    qk_d = {p: _dot_nt(gq_[p], gk[p]) * gdec[p] for p in pairs}
    a_mat = {p: jnp.where(strict, gdec[p] * kk[p], 0.0) * gb[p] for p in pairs}
    t_inv = dict(zip(pairs, _unit_lower_inverse([a_mat[p] for p in pairs], c,
                                                blk=SLOT if sample else 2 * SUBLANES, merge=not sample)))
    ge = {(i, h): col(ecum[i], SM_GA + h) for i, h in pairs}
    rhs = {p: jnp.concatenate([gv[p] * gb[p], gk[p] * (gb[p] * ge[p])], axis=1) for p in pairs}
    x = {p: _dot(t_inv[p], rhs[p]) for p in pairs}
    ktl = {(i, h): gk[i, h] * col(tail[i], SM_GA + h) for i, h in pairs}
    gqe = {p: gq_[p] * ge[p] for p in pairs}

    if sample:
        slots = [pl.ds(i * spg, spg) for i in groups]
        last = [ecl[i].reshape(spg, SLOT, 128)[:, SLOT - 1:SLOT, :] for i in groups]
        s_all = {(i, h): st_gdn[slots[i], h] for i, h in pairs}
        m = {p: jnp.concatenate([expand(x[p][:, hd:2 * hd]), expand(gqe[p])], axis=0) for p in pairs}
        rr = {p: _dot(m[p], s_all[p].reshape(spg * hd, hd)) for p in pairs}
        delta = {p: x[p][:, 0:hd] - rr[p][0:c] for p in pairs}
        og = {p: rr[p][c:2 * c] + _dot(qk_d[p], delta[p]) for p in pairs}
        un = {p: _dot_tn(expand(ktl[p]), delta[p]) for p in pairs}
        h_all = {(i, h): [st_ssd[i * spg + g, h] for g in range(spg)] for i, h in pairs}
        yi = {p: _dot_nt(expand(cq[p]), jnp.concatenate(h_all[p], axis=1)) for p in pairs}
        for i, h in pairs:
            rs = grp_rows[i]
            o_gdn[slots[i], h] = (s_all[i, h] * last[i][:, :, SM_GA + h:SM_GA + h + 1]
                                  + un[i, h].reshape(spg, hd, hd))
            s_y[rs, 512 + h * hd:512 + (h + 1) * hd] = og[i, h]
            for g in range(spg):
                o_ssd[i * spg + g, h] = (h_all[i, h][g] * last[i][g, :, SM_DT + h:SM_DT + h + 1]
                                         + upd[i, h][:, g * SSD_STATE:(g + 1) * SSD_STATE])
            s_y[rs, h * SSD_HEADDIM:(h + 1) * SSD_HEADDIM] = y_ssd[i, h] + yi[i, h]
    else:
        fg = {p: _dot_tn(ktl[p], x[p]) for p in pairs}
        qo = {p: _dot(qk_d[p], x[p]) for p in pairs}
        for i in groups:
            rs = grp_rows[i]
            s0 = [o_gdn[0, h] for h in heads]
            h0 = [o_ssd[0, h] for h in heads]
            m = [jnp.concatenate([gqe[i, h] - qo[i, h][:, hd:2 * hd], fg[i, h][:, hd:2 * hd]], axis=0) for h in heads]
            rr = [_dot(m[h], s0[h]) for h in heads]
            yi = [_dot_nt(cq[i, h], h0[h]) for h in heads]
            for h in heads:
                o_gdn[0, h] = s0[h] * col(ecl[i], SM_GA + h) - rr[h][c:c + hd] + fg[i, h][:, 0:hd]
                s_y[rs, 512 + h * hd:512 + (h + 1) * hd] = rr[h][0:c] + qo[i, h][:, 0:hd]
                o_ssd[0, h] = h0[h] * col(ecl[i], SM_DT + h) + upd[i, h]
                s_y[rs, h * SSD_HEADDIM:(h + 1) * SSD_HEADDIM] = y_ssd[i, h] + yi[h]

    cs = SLOT if sample else GLA_CHUNK
    n_sub, sh = tl // cs, cs.bit_length() - 1
    rt = lax.broadcasted_iota(jnp.int32, (tl, tl), 0)
    ct = lax.broadcasted_iota(jnp.int32, (tl, tl), 1)
    sub_same = (rt >> sh) == (ct >> sh)
    m_cum = sub_same & (ct <= rt)
    m_ref = sub_same & ((ct & (cs - 1)) < cs // 2)
    m3 = jnp.concatenate([m_cum.astype(BF16), m_ref.astype(BF16), sub_same.astype(BF16)], axis=0)
    lg = s_lg[...]
    c3 = _dot_exact01(m3, lg)
    cumg, refg, clg = c3[0:tl], c3[tl:2 * tl], c3[2 * tl:3 * tl]
    gla_q = zin[:, Z_Q:Z_Q + 128] * (GLA_DK ** -0.5)
    gla_k = masked(zin[:, Z_K:Z_K + 128])
    gla_v = zin[:, Z_V:Z_V + 256]
    qe, ke = gla_q * jnp.exp(cumg - refg), gla_k * jnp.exp(refg - cumg)
    qd, kt = gla_q * jnp.exp(cumg), gla_k * jnp.exp(clg - cumg)
    klane = lax.broadcasted_iota(jnp.int32, (1, 128), 1) >> 5
    vlane = lax.broadcasted_iota(jnp.int32, (1, 256), 1) >> 6
    sc = [jnp.where(m_cum, _dot_nt(jnp.where(klane == h, qe, 0.0), ke), 0.0) for h in heads]
    of = [_dot(sc[h], gla_v) for h in heads]
    o_gl = jnp.where(vlane == 0, of[0], 0.0)
    for h in range(1, GLA_HEADS):
        o_gl = o_gl + jnp.where(vlane == h, of[h], 0.0)

    wide = n_sub * 128

    def expand_sub(x):
        rr_ = lax.broadcasted_iota(jnp.int32, (tl, wide), 0) >> sh
        cc_ = lax.broadcasted_iota(jnp.int32, (tl, wide), 1) >> 7
        return jnp.where(rr_ == cc_, jnp.concatenate([x] * n_sub, axis=1), 0.0)

    head_diag = (((lax.broadcasted_iota(jnp.int32, (wide, 256), 0) >> 5) & 3)
                 == (lax.broadcasted_iota(jnp.int32, (wide, 256), 1) >> 6))
    u_all = jnp.where(head_diag, _dot_tn(expand_sub(kt), gla_v), 0.0)
    msub = ((lax.broadcasted_iota(jnp.int32, (tl, n_sub), 0) >> sh)
            == lax.broadcasted_iota(jnp.int32, (tl, n_sub), 1)).astype(BF16)
    dcol = jnp.exp(_dot_exact01_tn(lg, msub))
    if sample:
        w2 = st_gla[...].reshape(n_sub * 128, GLA_DV)
        e4 = ((lax.broadcasted_iota(jnp.int32, (GLA_DV, 256), 1) & (GLA_DV - 1))
              == lax.broadcasted_iota(jnp.int32, (GLA_DV, 256), 0)).astype(BF16)
        w_all = jnp.where(head_diag, _dot_x_exact01(w2, e4), 0.0)
    else:
        ws = []
        w = w_scr[...]
        for i in range(n_sub):
            ws.append(w)
            w = w * dcol[:, i:i + 1] + u_all[i * 128:(i + 1) * 128, :]
        w_scr[...] = w
        w_all = jnp.concatenate(ws, axis=0)
        for h in heads:
            o_gla[0, h] = w[h * GLA_DK:(h + 1) * GLA_DK, h * GLA_DV:(h + 1) * GLA_DV]
    s_y[:, 768:1024] = o_gl + _dot(expand_sub(qd), w_all)
    if sample:
        for i in range(n_sub):
            w = w_all[i * 128:(i + 1) * 128, :] * dcol[:, i:i + 1] + u_all[i * 128:(i + 1) * 128, :]
            for h in heads:
                o_gla[i, h] = w[h * GLA_DK:(h + 1) * GLA_DK, h * GLA_DV:(h + 1) * GLA_DV]

    e256 = e01[0:256, 0:256]
    y = s_y[:, 0:256] * _silu(zin[:, Z_Z:Z_Z + 256])
    ymix[:, 0:256] = masked(_rms(y, ssd_nrm[...]))
    o = s_y[:, 512:768]
    o = o * lax.rsqrt(_head_meansq(o, e256, GDN_HEADDIM) + EPS) * gdn_nrm[...]
    ymix[:, 512:768] = masked(o * _silu(zin[:, Z_GZ:Z_GZ + 256]))
    o = s_y[:, 768:1024]
    o = o * lax.rsqrt(_head_meansq(o, e256, GLA_DV) + EPS) * gla_nrm[...]
    ymix[:, 768:1024] = masked(o * _silu(zin[:, Z_GG:Z_GG + 256]))


def _slot_move_matrices(tl, rows):
    slots = tl // SLOT
    put = np.zeros((tl, rows * slots), np.float32)
    get = np.zeros((rows * slots, tl), np.float32)
    for g in range(slots):
        for j in range(rows):
            put[g * SLOT + SLOT_OFF - rows + j, j * slots + g] = 1.0
            get[j * slots + g, g * SLOT + SLOT - rows + j] = 1.0
    return jnp.asarray(put, BF16), jnp.asarray(get, BF16)


def _mixers(zin, l, prm, *, batch, rows_per_seq, sample, states=None, carry=None):
    t = zin.shape[0]
    tl = min(SAMPLE_MIX_TILE if sample else TILE_TOKENS, t)
    nseq = t // SLOT if sample else batch
    nrow = CONV_W - 1
    dims = [(SSD_HEADS, SSD_HEADDIM, SSD_STATE), (GDN_HEADS, GDN_HEADDIM, GDN_HEADDIM), (GLA_HEADS, GLA_DK, GLA_DV)]
    if sample:
        slots = tl // SLOT
        grid = (t // tl,)
        tok = lambda w: pl.BlockSpec((tl, w), lambda i: (i, 0))
        st_spec = lambda d: pl.BlockSpec((None, slots) + d, lambda i: (l, i, 0, 0, 0))
        st_shape = lambda d: jax.ShapeDtypeStruct((DEPTH, nseq) + d, F32)
        cv_in = lambda w: pl.BlockSpec((None, nrow, slots, w), lambda i: (l, 0, i, 0))
        row_in = lambda w: pl.BlockSpec((None, slots, w), lambda i: (l, i, 0))
        cv_spec = pl.BlockSpec((nrow, slots, 768), lambda i: (0, i, 0))
        cv_shape = jax.ShapeDtypeStruct((nrow, nseq, 768), F32)
        s5_spec = pl.BlockSpec((slots, S5_WIDTH), lambda i: (i, 0))
        s5_shape = jax.ShapeDtypeStruct((nseq, S5_WIDTH), F32)
    else:
        tiles = rows_per_seq // tl
        grid = (batch, tiles)
        tok = lambda w: pl.BlockSpec((tl, w), lambda b, j: (b * tiles + j, 0))
        st_spec = lambda d: pl.BlockSpec((1,) + d, lambda b, j: (b, 0, 0, 0))
        st_shape = lambda d: jax.ShapeDtypeStruct((batch,) + d, F32)
        cv_spec = pl.BlockSpec((1, nrow, 768), lambda b, j: (b, 0, 0))
        cv_shape = jax.ShapeDtypeStruct((batch, nrow, 768), F32)
        s5_spec = pl.BlockSpec((1, 1, S5_WIDTH), lambda b, j: (b, 0, 0))
        s5_shape = jax.ShapeDtypeStruct((batch, 1, S5_WIDTH), F32)
    names = ["psmall", "ssd_cw", "ssd_cb", "ssd_nrm", "s5_b", "s5_ct", "s5_l1", "s5_ps" if sample else "s5_pp",
             "s5_d", "glu_w", "s5_nrm", "gdn_cw", "gdn_nrm", "gla_w2", "gla_bg", "gla_nrm"]
    consts = [prm[n] for n in names]
    in_specs = [tok(NZ)] + [_layer_spec(a, l, len(grid)) for a in consts] + [_shared_spec(prm["e01"])]
    operands = [zin] + consts + [prm["e01"]]
    aliases = {}
    if sample:
        put3, get3 = _slot_move_matrices(tl, nrow)
        _, get1 = _slot_move_matrices(tl, 1)
        in_specs += [st_spec(dims[0]), cv_in(SSD_CONV_DIM), row_in(S5_WIDTH), row_in(S5_WIDTH),
                     st_spec(dims[1]), cv_in(GDN_CONV_DIM), st_spec(dims[2]),
                     _shared_spec(put3), _shared_spec(get3), _shared_spec(get1)]
        operands += list(states) + [put3, get3, get1]
        if carry is not None:
            aliases = {len(operands) + k: out_idx for k, out_idx in enumerate((1, 5, 7))}
            in_specs += [pl.BlockSpec(memory_space=pl.ANY)] * 3
            operands += list(carry)
    out_specs = [tok(D_MODEL), st_spec(dims[0]), cv_spec, s5_spec, s5_spec, st_spec(dims[1]), cv_spec,
                 st_spec(dims[2])]
    out_shape = [jax.ShapeDtypeStruct((t, D_MODEL), F32), st_shape(dims[0]), cv_shape, s5_shape, s5_shape,
                 st_shape(dims[1]), cv_shape, st_shape(dims[2])]
    scratch = [pltpu.VMEM((tl + SUBLANES, 768), F32), pltpu.VMEM((tl + SUBLANES, 768), F32),
               pltpu.VMEM((tl, 768), F32), pltpu.VMEM((tl, 128), F32), pltpu.VMEM((tl, 128), F32),
               pltpu.VMEM((tl, 128), F32), pltpu.VMEM((tl, 768), F32), pltpu.VMEM((tl, 128), F32),
               pltpu.VMEM((tl, D_MODEL), F32), pltpu.VMEM((tl, S5_WIDTH), F32), pltpu.VMEM((tl, S5_WIDTH), F32),
               pltpu.VMEM((GLA_HEADS * GLA_DK, GLA_HEADS * GLA_DV), F32)]
    return pl.pallas_call(
        functools.partial(_mixers_kernel, tl=tl, sample=sample, n_carry=len(aliases)),
        grid=grid, in_specs=in_specs, out_specs=out_specs, out_shape=out_shape, scratch_shapes=scratch,
        input_output_aliases=aliases,
        compiler_params=pltpu.CompilerParams(dimension_semantics=("arbitrary",) * len(grid),
                                             vmem_limit_bytes=VMEM_LIMIT),
        name="mixers_sample" if sample else "mixers_prompt",
    )(*operands)


def _outffn_kernel(*refs, tm, sample, final):
    x_ref, y_ref, g1_ref, sh_ref, sc_ref, g2_ref, nf_ref, wout, up, cw, cb, down = refs[:12]
    k = 12
    hist = put2 = get2 = fin = None
    if sample:
        hist, put2, get2 = refs[k:k + 3]
        k += 3
    if final:
        fin = refs[k]
        k += 1
    x2_ref, cv_ref, ext = refs[k], refs[k + 1], refs[k + 2]

    if sample:
        ex = lambda r: _expand_rows(r[...], SLOT)
        g1, sh, sc, g2 = ex(g1_ref), ex(sh_ref), ex(sc_ref), ex(g2_ref)
        valid = (lax.broadcasted_iota(jnp.int32, (tm, 1), 0) & (SLOT - 1)) >= SLOT_OFF
        ext[0:SUBLANES, :] = jnp.zeros((SUBLANES, 2 * D_FF), F32)
    else:
        g1, sh, sc, g2 = g1_ref[0], sh_ref[0], sc_ref[0], g2_ref[0]

        @pl.when(pl.program_id(1) == 0)
        def _():
            ext[0:SUBLANES, :] = jnp.zeros((SUBLANES, 2 * D_FF), F32)

    x1 = x_ref[...] + g1 * _dot(y_ref[...], wout[...])
    h = (_rms(x1, nf_ref[...]) * (1.0 + sc) + sh).astype(BF16)
    for j in range(0, 2 * D_FF, FF_CHUNK):
        uj = jnp.dot(h, up[:, j:j + FF_CHUNK], preferred_element_type=F32)
        if sample:
            hj = jnp.concatenate([hist[:, r, j:j + FF_CHUNK] for r in range(FFN_CONV_W - 1)], axis=0)
            uj = jnp.where(valid, uj, _place_rows(put2[...], hj))
        ext[pl.ds(SUBLANES, tm), j:j + FF_CHUNK] = uj

    def conv(col):
        acc = cb[:, col:col + FF_CHUNK]
        for kk in range(FFN_CONV_W):
            acc = acc + cw[kk:kk + 1, col:col + FF_CHUNK] * ext[pl.ds(SUBLANES - (FFN_CONV_W - 1) + kk, tm),
                                                                 col:col + FF_CHUNK]
        return acc

    f = jnp.zeros((tm, D_MODEL), F32)
    for j in range(0, D_FF, FF_CHUNK):
        act = _silu(conv(j)) * conv(D_FF + j)
        f = f + _dot(act, down[j:j + FF_CHUNK, :])
    x2 = x1 + g2 * f
    if final:
        x2 = _rms(x2, fin[...])
    if sample:
        x2_ref[...] = jnp.where(valid, x2, 0.0)
        for j in range(0, 2 * D_FF, FF_CHUNK):
            got = _place_rows(get2[...], ext[pl.ds(SUBLANES, tm), j:j + FF_CHUNK])
            for r in range(FFN_CONV_W - 1):
                cv_ref[:, r, j:j + FF_CHUNK] = got[r * (tm // SLOT):(r + 1) * (tm // SLOT)]
    else:
        x2_ref[...] = x2
        tail = ext[pl.ds(tm + SUBLANES - (FFN_CONV_W - 1), FFN_CONV_W - 1), :]
        cv_ref[0] = tail
        ext[pl.ds(SUBLANES - (FFN_CONV_W - 1), FFN_CONV_W - 1), :] = tail


def _out_ffn(x2d, ymix, mod, l, prm, *, batch, rows_per_seq, sample, hist=None, final_gain=None):
    t = x2d.shape[0]
    tm = min(SAMPLE_FFN_TILE if sample else TILE_TOKENS, t)
    final = final_gain is not None
    nrows = FFN_CONV_W - 1
    if sample:
        slots = tm // SLOT
        grid = (t // tm,)
        tok = lambda w: pl.BlockSpec((tm, w), lambda i: (i, 0))
        mspecs = [pl.BlockSpec((None, slots, D_MODEL), (lambda i, j=j: (l, i, j))) for j in (2, 3, 4, 5)]
        mops = [mod] * 4
        cv_spec = pl.BlockSpec((slots, nrows, 2 * D_FF), lambda i: (i, 0, 0))
        cv_shape = jax.ShapeDtypeStruct((t // SLOT, nrows, 2 * D_FF), F32)
    else:
        tiles = rows_per_seq // tm
        grid = (batch, tiles)
        tok = lambda w: pl.BlockSpec((tm, w), lambda b, j: (b * tiles + j, 0))
        m3 = mod.reshape(DEPTH * batch * 6, 1, D_MODEL)
        mspecs = [pl.BlockSpec((1, 1, D_MODEL), (lambda b, i, j=j: ((l * batch + b) * 6 + j, 0, 0))) for j in (2, 3, 4, 5)]
        mops = [m3] * 4
        cv_spec = pl.BlockSpec((1, nrows, 2 * D_FF), lambda b, j: (b, 0, 0))
        cv_shape = jax.ShapeDtypeStruct((batch, nrows, 2 * D_FF), F32)
    consts = [prm[n] for n in ("norm_ffn", "w_out", "ffn_up", "ffn_cw", "ffn_cb", "ffn_down")]
    in_specs = [tok(D_MODEL), tok(D_MODEL)] + mspecs + [_layer_spec(a, l, len(grid)) for a in consts]
    operands = [x2d, ymix] + mops + consts
    if sample:
        put2, get2 = _slot_move_matrices(tm, nrows)
        in_specs += [pl.BlockSpec((None, slots, nrows, 2 * D_FF), lambda i: (l, i, 0, 0)),
                     _shared_spec(put2), _shared_spec(get2)]
        operands += [hist, put2, get2]
    if final:
        in_specs.append(_shared_spec(final_gain))
        operands.append(final_gain)
    return pl.pallas_call(
        functools.partial(_outffn_kernel, tm=tm, sample=sample, final=final),
        grid=grid, in_specs=in_specs,
        out_specs=[tok(D_MODEL), cv_spec],
        out_shape=[jax.ShapeDtypeStruct((t, D_MODEL), F32), cv_shape],
        scratch_shapes=[pltpu.VMEM((tm + SUBLANES, 2 * D_FF), F32)],
        compiler_params=pltpu.CompilerParams(dimension_semantics=("arbitrary",) * len(grid),
                                             vmem_limit_bytes=VMEM_LIMIT),
        name="out_ffn_sample" if sample else "out_ffn_prompt",
    )(*operands)


def _reorder_w_in(w_in):
    o = np.cumsum([0, 256, 768, 4, 256, 768, 256, 4, 4, 128, 128, 256, 256, 16])
    seg = lambda i: w_in[:, :, o[i]:o[i + 1]]
    small = [seg(2), seg(6), seg(7), seg(12)]
    pad = jnp.zeros(w_in.shape[:2] + (128 - 28,), w_in.dtype)
    cols = [seg(1), seg(4), seg(0), seg(3), seg(5), seg(10), seg(11), seg(8), seg(9)] + small + [pad]
    return jnp.concatenate(cols, axis=-1).astype(BF16)


def _pad_lanes(a, lane0, width=128):
    return jnp.pad(a, ((0, 0), (lane0, width - lane0 - a.shape[1])))[:, None, :]


def kernel(x_prompt, x_sample, c_prompt, c_sample, state_ssd, state_ssd_conv, state_s5_re, state_s5_im,
           state_gdn, state_gdn_conv, state_gla, state_ffn_conv, ada_w, ada_b, norm_mix, norm_ffn, w_in, w_out,
           ssd_conv_w, ssd_conv_b, ssd_dt_bias, ssd_a_log, ssd_d, ssd_norm, s5_a_re, s5_a_im, s5_log_dt,
           s5_b_re, s5_b_im, s5_c_re, s5_c_im, s5_d, s5_glu_w, s5_norm, gdn_conv_w, gdn_a_log, gdn_dt_bias,
           gdn_norm, gla_wg2, gla_bg, gla_norm, ffn_up, ffn_conv_w, ffn_conv_b, ffn_down, final_norm):
    nb, seq = x_prompt.shape[0], x_prompt.shape[1]
    ns, dseq = x_sample.shape[0], x_sample.shape[1]
    assert dseq == SLOT - SLOT_OFF

    mod_p, mod_s = _modulation(c_prompt.astype(F32), c_sample.astype(F32), ada_w, ada_b)
    s5_b, s5_ct, s5_l1, s5_pp, s5_ps = _s5_prepare(s5_a_re, s5_a_im, s5_log_dt, s5_b_re, s5_b_im, s5_c_re, s5_c_im)

    row = lambda a: a[:, None, :]
    psmall = jnp.concatenate([
        _pad_lanes(ssd_dt_bias, SM_DT) + _pad_lanes(gdn_dt_bias, SM_GA),
        _pad_lanes(ssd_a_log, SM_DT) + _pad_lanes(gdn_a_log, SM_GA),
        _pad_lanes(ssd_d, 0), jnp.zeros((DEPTH, SUBLANES - 3, 128), F32)], axis=1)
    prm = dict(
        psmall=psmall, ssd_cw=ssd_conv_w, ssd_cb=row(ssd_conv_b), ssd_nrm=row(ssd_norm),
        s5_b=s5_b, s5_ct=s5_ct, s5_l1=s5_l1, s5_pp=s5_pp, s5_ps=s5_ps, s5_d=row(s5_d),
        glu_w=s5_glu_w.astype(BF16), s5_nrm=row(s5_norm), gdn_cw=gdn_conv_w,
        gdn_nrm=row(jnp.tile(gdn_norm, (1, GDN_HEADS))),
        gla_w2=jnp.pad(gla_wg2, ((0, 0), (SM_LR, 128 - SM_LR - GLA_GATE_RANK), (0, 0))).astype(BF16),
        gla_bg=row(gla_bg), gla_nrm=row(jnp.tile(gla_norm, (1, GLA_HEADS))),
        e01=jnp.asarray(np.arange(512)[:, None] // 64 == np.arange(512)[None, :] // 64, BF16),
        norm_mix=row(norm_mix), norm_ffn=row(norm_ffn), w_in=_reorder_w_in(w_in), w_out=w_out.astype(BF16),
        ffn_up=ffn_up.astype(BF16), ffn_cw=ffn_conv_w, ffn_cb=row(ffn_conv_b), ffn_down=ffn_down.astype(BF16))
    fin = final_norm.reshape(1, D_MODEL)

    xp = x_prompt.astype(F32).reshape(nb * seq, D_MODEL)
    p_new = []
    for l in range(DEPTH):
        zin = _input_projection(xp, mod_p, l, prm, rows_per_seq=seq, sample=False)
        ymix, h_ssd, cv_ssd, h5r, h5i, s_gdn, cv_gdn, s_gla = _mixers(zin, l, prm, batch=nb, rows_per_seq=seq,
                                                                      sample=False)
        xp, cv_ffn = _out_ffn(xp, ymix, mod_p, l, prm, batch=nb, rows_per_seq=seq, sample=False,
                              final_gain=fin if l == DEPTH - 1 else None)
        p_new.append((jnp.swapaxes(h_ssd, 2, 3), cv_ssd, h5r.reshape(nb, S5_GROUPS, S5_STATE),
                      h5i.reshape(nb, S5_GROUPS, S5_STATE), s_gdn, cv_gdn, s_gla, cv_ffn))

    xs = jnp.pad(x_sample.astype(F32), ((0, 0), (SLOT_OFF, 0), (0, 0))).reshape(ns * SLOT, D_MODEL)
    f32 = lambda a: a.astype(F32)
    states = (jnp.swapaxes(f32(state_ssd), 3, 4), jnp.swapaxes(f32(state_ssd_conv), 1, 2),
              f32(state_s5_re).reshape(DEPTH, ns, S5_WIDTH), f32(state_s5_im).reshape(DEPTH, ns, S5_WIDTH),
              f32(state_gdn), jnp.swapaxes(f32(state_gdn_conv), 1, 2), f32(state_gla))
    hist = f32(state_ffn_conv)
    s_new, carry = [], None
    for l in range(DEPTH):
        zin = _input_projection(xs, mod_s, l, prm, rows_per_seq=SLOT, sample=True)
        ymix, h_ssd, cv_ssd, h5r, h5i, s_gdn, cv_gdn, s_gla = _mixers(zin, l, prm, batch=ns, rows_per_seq=SLOT,
                                                                      sample=True, states=states, carry=carry)
        carry = (h_ssd, s_gdn, s_gla)
        xs, cv_ffn = _out_ffn(xs, ymix, mod_s, l, prm, batch=ns, rows_per_seq=SLOT, sample=True, hist=hist,
                              final_gain=fin if l == DEPTH - 1 else None)
        s_new.append((jnp.swapaxes(cv_ssd, 0, 1), h5r.reshape(ns, S5_GROUPS, S5_STATE),
                      h5i.reshape(ns, S5_GROUPS, S5_STATE), jnp.swapaxes(cv_gdn, 0, 1), cv_ffn))

    y_p = xp.reshape(nb, seq, D_MODEL).astype(x_prompt.dtype)
    y_s = xs.reshape(ns, SLOT, D_MODEL)[:, SLOT_OFF:, :].astype(x_sample.dtype)
    p_st = [jnp.stack(t) for t in zip(*p_new)]
    s_cv_ssd, s_5r, s_5i, s_cv_gdn, s_cv_ffn = [jnp.stack(t) for t in zip(*s_new)]
    s_ssd, s_gdn, s_gla = carry
    return (y_p, y_s, *p_st,
            jnp.swapaxes(s_ssd, 3, 4), s_cv_ssd, s_5r, s_5i, s_gdn, s_cv_gdn, s_gla, s_cv_ffn)
```

```python
import functools

import numpy as np
import jax
import jax.numpy as jnp
from jax import lax
from jax.experimental import pallas as pl
from jax.experimental.pallas import tpu as pltpu

F32 = jnp.float32
BF16 = jnp.bfloat16

D_MODEL = 1024
DEPTH = 2
GROUP_WIDTH = 256
CONV_W = 4
SSD_HEADS = 4
SSD_HEADDIM = 64
SSD_STATE = 128
SSD_CONV_DIM = 768
S5_GROUPS = 16
S5_GROUP_CH = 16
S5_STATE = 64
S5_WIDTH = S5_GROUPS * S5_STATE
GDN_HEADS = 4
GDN_HEADDIM = 64
GDN_CONV_DIM = 768
GLA_HEADS = 4
GLA_DK = 32
GLA_DV = 64
GLA_GATE_RANK = 16
GLA_TAU = 16.0
D_FF = 2816
FFN_CONV_W = 3
EPS = 1e-6

NZ = 3200
Z_XBC, Z_QKV, Z_Z, Z_U, Z_GZ, Z_V, Z_GG, Z_Q, Z_K, Z_SM = 0, 768, 1536, 1792, 2048, 2304, 2560, 2816, 2944, 3072
SM_DT, SM_GA, SM_GB, SM_LR = 0, 4, 8, 12

SUBLANES = 8
SLOT = SUBLANES
SLOT_OFF = SLOT - 4
SCAN_CHUNK = 64
GLA_CHUNK = 16
TILE_TOKENS = 256
SAMPLE_MIX_TILE = 128
SAMPLE_FFN_TILE = 128
FF_CHUNK = 256
PROJ_COLS = 256
VMEM_LIMIT = 56 * 1024 * 1024


def _silu(x):
    return x * (1.0 / (1.0 + jnp.exp(-x)))


def _sigmoid(x):
    return 1.0 / (1.0 + jnp.exp(-x))


def _softplus(x):
    return jnp.maximum(x, 0.0) + jnp.log(1.0 + jnp.exp(-jnp.abs(x)))


def _dot(a, b):
    return jnp.dot(a.astype(BF16), b.astype(BF16), preferred_element_type=F32)


def _dot_nt(a, b):
    return lax.dot_general(a.astype(BF16), b.astype(BF16), (((1,), (1,)), ((), ())), preferred_element_type=F32)


def _dot_tn(a, b):
    return lax.dot_general(a.astype(BF16), b.astype(BF16), (((0,), (0,)), ((), ())), preferred_element_type=F32)


def _split_hi_lo(x):
    hi = x.astype(BF16)
    lo = (x - hi.astype(F32)).astype(BF16)
    return hi, lo


def _dot_exact01(m01, x):
    hi, lo = _split_hi_lo(x)
    return jnp.dot(m01, hi, preferred_element_type=F32) + jnp.dot(m01, lo, preferred_element_type=F32)


def _dot_exact01_tn(x, m01):
    hi, lo = _split_hi_lo(x)
    dn = (((0,), (0,)), ((), ()))
    return (lax.dot_general(hi, m01, dn, preferred_element_type=F32)
            + lax.dot_general(lo, m01, dn, preferred_element_type=F32))


def _place_rows(p01, x):
    hi = x.astype(BF16)
    rest = x - hi.astype(F32)
    mid = rest.astype(BF16)
    lo = (rest - mid.astype(F32)).astype(BF16)
    mm = lambda v: jnp.dot(p01, v, preferred_element_type=F32)
    return (mm(hi) + mm(mid)) + mm(lo)


def _dot_x_exact01(x, m01):
    hi, lo = _split_hi_lo(x)
    return jnp.dot(hi, m01, preferred_element_type=F32) + jnp.dot(lo, m01, preferred_element_type=F32)


def _unit_lower_inverse(mats, c, blk, merge, between):
    r = lax.broadcasted_iota(jnp.int32, (c, c), 0)
    s = lax.broadcasted_iota(jnp.int32, (c, c), 1)
    sh = blk.bit_length() - 1
    diag = (r >> sh) == (s >> sh)
    eye = jnp.where(r == s, 1.0, 0.0)
    p = [jnp.where(diag, a, 0.0) for a in mats]
    t = [eye - d for d in p]
    pw = 2
    while pw < blk:
        p = [_dot(d, d) for d in p]
        between()
        t = [m + _dot(m, d) for m, d in zip(t, p)]
        between()
        pw *= 2
    size = blk
    while merge and size < c:
        sh = size.bit_length() - 1
        pair = ((r >> (sh + 1)) == (s >> (sh + 1))) & ((r >> sh) != (s >> sh))
        lt = [_dot(jnp.where(pair, a, 0.0), m) for a, m in zip(mats, t)]
        between()
        t = [m - _dot(m, y) for m, y in zip(t, lt)]
        between()
        size *= 2
    return t


def _rms(x, gain):
    return x * lax.rsqrt(jnp.mean(x * x, axis=-1, keepdims=True) + EPS) * gain


def _expand_rows(m, reps):
    g, n = m.shape
    return jnp.broadcast_to(m[:, None, :], (g, reps, n)).reshape(g * reps, n)


def _mod_kernel(cp_ref, cs_ref, w_ref, b_ref, op_ref, os_ref):
    w = w_ref[0].astype(BF16)
    op_ref[0] = _dot(_silu(cp_ref[...]), w) + b_ref[0]
    os_ref[0] = _dot(_silu(cs_ref[...]), w) + b_ref[0]


def _modulation(c_prompt, c_sample, ada_w, ada_b):
    nb, ns = c_prompt.shape[0], c_sample.shape[0]
    tn = 1024
    return pl.pallas_call(
        _mod_kernel,
        grid=(DEPTH, 6 * D_MODEL // tn),
        in_specs=[pl.BlockSpec((nb, D_MODEL), lambda l, j: (0, 0)),
                  pl.BlockSpec((ns, D_MODEL), lambda l, j: (0, 0)),
                  pl.BlockSpec((1, D_MODEL, tn), lambda l, j: (l, 0, j)),
                  pl.BlockSpec((1, 1, tn), lambda l, j: (l, 0, j))],
        out_specs=[pl.BlockSpec((1, nb, tn), lambda l, j: (l, 0, j)),
                   pl.BlockSpec((1, ns, tn), lambda l, j: (l, 0, j))],
        out_shape=[jax.ShapeDtypeStruct((DEPTH, nb, 6 * D_MODEL), F32),
                   jax.ShapeDtypeStruct((DEPTH, ns, 6 * D_MODEL), F32)],
        compiler_params=pltpu.CompilerParams(dimension_semantics=("arbitrary", "arbitrary"),
                                             vmem_limit_bytes=VMEM_LIMIT),
        name="adaln_mod",
    )(c_prompt, c_sample, ada_w, ada_b.reshape(DEPTH, 1, 6 * D_MODEL))


def _s5_prep_kernel(are_ref, aim_ref, ldt_ref, btr_ref, bti_ref, cr_ref, ci_ref, arow_ref, e_ref, gm_ref,
                    b_out, c_out, l1_out, pp_out, ps_out):
    a_re, a_im = are_ref[0], aim_ref[0]
    dt = jnp.exp(ldt_ref[0])
    mag = jnp.exp(dt * a_re)
    ab_re, ab_im = mag * jnp.cos(dt * a_im), mag * jnp.sin(dt * a_im)
    den = a_re * a_re + a_im * a_im
    f_re = ((ab_re - 1.0) * a_re + ab_im * a_im) / den
    f_im = (ab_im * a_re - (ab_re - 1.0) * a_im) / den
    bb_re = f_re * btr_ref[0] - f_im * bti_ref[0]
    bb_im = f_re * bti_ref[0] + f_im * btr_ref[0]
    e01, gm = e_ref[...], gm_ref[...]
    b_out[0, 0] = (_dot_x_exact01(bb_re, e01) * gm).astype(BF16)
    b_out[0, 1] = (_dot_x_exact01(bb_im, e01) * gm).astype(BF16)
    c_out[0, 0] = (_dot_x_exact01(cr_ref[0], e01) * gm).astype(BF16)
    c_out[0, 1] = (_dot_x_exact01(ci_ref[0], e01) * gm).astype(BF16)
    ar, ai, dtr = arow_ref[0, 0:1, :], arow_ref[0, 1:2, :], jnp.exp(arow_ref[0, 2:3, :])
    row = lax.broadcasted_iota(jnp.int32, (SUBLANES, 1), 0)

    def power(k):
        m = jnp.exp(k * dtr * ar)
        return m * jnp.cos(k * dtr * ai), m * jnp.sin(k * dtr * ai)

    for i, d in enumerate((1, 2, 4)):
        pr, pi = power(jnp.full((SUBLANES, 1), float(d), F32))
        keep = row >= d
        l1_out[0, i] = jnp.where(keep, pr, 0.0)
        l1_out[0, 3 + i] = jnp.where(keep, pi, 0.0)
    pr, pi = power((row + 1).astype(F32))
    pp_out[0, 0], pp_out[0, 1] = pr, pi
    pr, pi = power(jnp.maximum(row - (SLOT_OFF - 1), 0).astype(F32))
    ps_out[0, 0] = jnp.where(row >= SLOT_OFF, pr, 0.0)
    ps_out[0, 1] = jnp.where(row >= SLOT_OFF, pi, 0.0)


def _s5_prepare(s5_a_re, s5_a_im, s5_log_dt, s5_b_re, s5_b_im, s5_c_re, s5_c_im):
    rows = S5_GROUPS * S5_GROUP_CH
    rep = lambda a: jnp.repeat(a, S5_GROUP_CH, axis=1)
    ldt = jnp.broadcast_to(s5_log_dt[:, :, None], (DEPTH, S5_GROUPS, S5_STATE))
    bt = lambda b: jnp.swapaxes(b, 2, 3).reshape(DEPTH, rows, S5_STATE)
    arow = jnp.stack([s5_a_re.reshape(DEPTH, S5_WIDTH), s5_a_im.reshape(DEPTH, S5_WIDTH),
                      ldt.reshape(DEPTH, S5_WIDTH)], axis=1)
    arow = jnp.pad(arow, ((0, 0), (0, SUBLANES - 3), (0, 0)))
    e01 = np.zeros((S5_STATE, S5_WIDTH), np.float32)
    e01[np.arange(S5_WIDTH) % S5_STATE, np.arange(S5_WIDTH)] = 1.0
    gm = (np.arange(rows)[:, None] // S5_GROUP_CH == np.arange(S5_WIDTH)[None, :] // S5_STATE).astype(np.float32)
    p3 = lambda: pl.BlockSpec((1, rows, S5_STATE), lambda l: (l, 0, 0))
    tab = lambda n: pl.BlockSpec((1, n, SUBLANES, S5_WIDTH), lambda l: (l, 0, 0, 0))
    return pl.pallas_call(
        _s5_prep_kernel,
        grid=(DEPTH,),
        in_specs=[p3(), p3(), p3(), p3(), p3(), p3(), p3(),
                  pl.BlockSpec((1, SUBLANES, S5_WIDTH), lambda l: (l, 0, 0)),
                  pl.BlockSpec((S5_STATE, S5_WIDTH), lambda l: (0, 0)),
                  pl.BlockSpec((rows, S5_WIDTH), lambda l: (0, 0))],
        out_specs=[pl.BlockSpec((1, 2, rows, S5_WIDTH), lambda l: (l, 0, 0, 0)),
                   pl.BlockSpec((1, 2, rows, S5_WIDTH), lambda l: (l, 0, 0, 0)),
                   tab(6), tab(2), tab(2)],
        out_shape=[jax.ShapeDtypeStruct((DEPTH, 2, rows, S5_WIDTH), BF16),
                   jax.ShapeDtypeStruct((DEPTH, 2, rows, S5_WIDTH), BF16),
                   jax.ShapeDtypeStruct((DEPTH, 6, SUBLANES, S5_WIDTH), F32),
                   jax.ShapeDtypeStruct((DEPTH, 2, SUBLANES, S5_WIDTH), F32),
                   jax.ShapeDtypeStruct((DEPTH, 2, SUBLANES, S5_WIDTH), F32)],
        compiler_params=pltpu.CompilerParams(dimension_semantics=("arbitrary",), vmem_limit_bytes=VMEM_LIMIT),
        name="s5_prepare",
    )(rep(s5_a_re), rep(s5_a_im), rep(ldt), bt(s5_b_re), bt(s5_b_im),
      s5_c_re.reshape(DEPTH, rows, S5_STATE), s5_c_im.reshape(DEPTH, rows, S5_STATE),
      arow, jnp.asarray(e01, BF16), jnp.asarray(gm, F32))


def _layer_spec(a, l, grid_rank):
    nd = a.ndim - 1
    return pl.BlockSpec((None,) + a.shape[1:], lambda *_: (l,) + (0,) * nd, pipeline_mode=pl.Buffered(1))


def _shared_spec(a):
    nd = a.ndim
    return pl.BlockSpec(a.shape, lambda *_: (0,) * nd, pipeline_mode=pl.Buffered(1))


def _head_meansq(x, e01, width):
    return _dot_x_exact01(x * x, e01) * (1.0 / width)


def _mixers_kernel(*refs, tl, sample, n_carry, tiles_per_seq):
    x_ref, sh_ref, sc_ref, gain_ref, w_in = refs[:5]
    (psm, ssd_cw, ssd_cb, ssd_nrm, s5_b, s5_ct, s5_l1, s5_p, s5_d, glu_w, s5_nrm,
     gdn_cw, gdn_nrm, gla_w2, gla_bg, gla_nrm, e01) = refs[5:22]
    n_in = 22
    if sample:
        st_ssd, hist_ssd, st_s5r, st_s5i, st_gdn, hist_gdn, st_gla, put3, get3, get1 = refs[22:32]
        n_in = 32 + n_carry
    outs = refs[n_in:n_in + 8]
    ymix, o_ssd, o_ssd_cv, o_s5r, o_s5i, o_gdn, o_gdn_cv, o_gla = outs
    (ext_ssd, ext_gdn, s_xbc, s_dt, s_la, s_beta, s_qkv, s_lg, s_y, s_hre, s_him, w_scr,
     zin, znext) = refs[n_in + 8:]

    step = pl.program_id(0)

    @pl.when(step == 0)
    def _():
        znext[...] = jnp.zeros(znext.shape, F32)

    zin[...] = znext[...]
    if sample:
        sh, sc = _expand_rows(sh_ref[...], SLOT), _expand_rows(sc_ref[...], SLOT)
    else:
        sh, sc = sh_ref[0], sc_ref[0]
    h_next = (_rms(x_ref[...], gain_ref[...]) * (1.0 + sc) + sh).astype(BF16)
    pending = list(range(0, NZ, PROJ_COLS))

    def fill(n=1):
        for _ in range(min(n, len(pending))):
            c0 = pending.pop(0)
            cols = slice(c0, min(c0 + PROJ_COLS, NZ))
            znext[:, cols] = jnp.dot(h_next, w_in[:, cols], preferred_element_type=F32)

    rows = lax.broadcasted_iota(jnp.int32, (tl, 1), 0)
    valid = (rows & (SLOT - 1)) >= SLOT_OFF if sample else None
    first = None if sample else (lax.rem(jnp.maximum(step - 1, 0), tiles_per_seq) == 0)

    def masked(x):
        return jnp.where(valid, x, 0.0) if sample else x

    if sample:
        ext_ssd[0:SUBLANES, :] = jnp.zeros((SUBLANES, SSD_CONV_DIM), F32)
        ext_gdn[0:SUBLANES, :] = jnp.zeros((SUBLANES, GDN_CONV_DIM), F32)
    else:
        @pl.when(first)
        def _():
            ext_ssd[0:SUBLANES, :] = jnp.zeros((SUBLANES, SSD_CONV_DIM), F32)
            ext_gdn[0:SUBLANES, :] = jnp.zeros((SUBLANES, GDN_CONV_DIM), F32)
            o_ssd[...] = jnp.zeros(o_ssd.shape, F32)
            o_gdn[...] = jnp.zeros(o_gdn.shape, F32)
            w_scr[...] = jnp.zeros(w_scr.shape, F32)
            o_s5r[...] = jnp.zeros(o_s5r.shape, F32)
            o_s5i[...] = jnp.zeros(o_s5i.shape, F32)

    def conv_in(ext_ref, col, hist_ref, w_ref, o_cv, finish):
        xin = zin[:, col:col + 768]
        if sample:
            hist = hist_ref[...]
            xin = jnp.where(valid, xin, _place_rows(put3[...], hist.reshape(-1, hist.shape[-1])))
            o_cv[...] = _place_rows(get3[...], xin).reshape(o_cv.shape)
        ext_ref[pl.ds(SUBLANES, tl), :] = xin
        for c0 in range(0, 768, 256):
            acc = None
            for k in range(CONV_W):
                term = w_ref[k:k + 1, c0:c0 + 256] * ext_ref[pl.ds(SUBLANES - (CONV_W - 1) + k, tl), c0:c0 + 256]
                acc = term if acc is None else acc + term
            finish(c0, acc)
            fill()
        if not sample:
            tail = ext_ref[pl.ds(tl + SUBLANES - (CONV_W - 1), CONV_W - 1), :]
            o_cv[0] = tail
            ext_ref[pl.ds(SUBLANES - (CONV_W - 1), CONV_W - 1), :] = tail

    def finish_ssd(c0, y):
        s_xbc[:, c0:c0 + 256] = _silu(y + ssd_cb[:, c0:c0 + 256])

    def finish_gdn(c0, y):
        y = _silu(y)
        if c0 < 512:
            inv = lax.rsqrt(_dot_x_exact01(y * y, e01[0:256, 0:256]) + EPS)
            y = y * inv * (GDN_HEADDIM ** -0.5) if c0 == 0 else y * inv
        s_qkv[:, c0:c0 + 256] = y

    fill()
    conv_in(ext_ssd, Z_XBC, hist_ssd if sample else None, ssd_cw, o_ssd_cv, finish_ssd)

    sm = zin[:, Z_SM:Z_SM + 128]
    sp = _softplus(sm + psm[0:1, :])
    s_dt[...] = masked(sp)
    s_la[...] = masked(-jnp.exp(psm[1:2, :]) * sp)
    s_beta[...] = masked(_sigmoid(sm))
    s_lg[...] = masked(-_softplus(-(_dot(sm, gla_w2[...]) + gla_bg[...])) * (1.0 / GLA_TAU))
    fill()

    conv_in(ext_gdn, Z_QKV, hist_gdn if sample else None, gdn_cw, o_gdn_cv, finish_gdn)

    u = zin[:, Z_U:Z_U + 256]
    n8 = tl // SUBLANES
    y5_parts = []
    side_steps = []

    def s5_scan(c0):
        cs_ = slice(c0, c0 + 256)
        x_re = masked(_dot(u, s5_b[0, :, cs_])).reshape(n8, SUBLANES, 256)
        x_im = masked(_dot(u, s5_b[1, :, cs_])).reshape(n8, SUBLANES, 256)
        for i, d in enumerate((1, 2, 4)):
            sr, si = pltpu.roll(x_re, d, 1), pltpu.roll(x_im, d, 1)
            ar, ai = s5_l1[i, :, cs_][None], s5_l1[3 + i, :, cs_][None]
            x_re, x_im = x_re + ar * sr - ai * si, x_im + ar * si + ai * sr
        if sample:
            c_re, c_im = st_s5r[:, cs_][:, None, :], st_s5i[:, cs_][:, None, :]
            pr, pi = s5_p[0, :, cs_][None], s5_p[1, :, cs_][None]
            x_re, x_im = x_re + pr * c_re - pi * c_im, x_im + pr * c_im + pi * c_re
        s_hre[:, cs_] = x_re.reshape(tl, 256)
        s_him[:, cs_] = x_im.reshape(tl, 256)

    def s5_carry(c0):
        cs_ = slice(c0, c0 + 256)
        pr, pi = s5_p[0, :, cs_], s5_p[1, :, cs_]
        c_re, c_im = o_s5r[0, :, cs_], o_s5i[0, :, cs_]
        for j in range(n8):
            rs = pl.ds(j * SUBLANES, SUBLANES)
            hr = s_hre[rs, cs_] + pr * c_re - pi * c_im
            hi = s_him[rs, cs_] + pr * c_im + pi * c_re
            s_hre[rs, cs_] = hr
            s_him[rs, cs_] = hi
            c_re, c_im = hr[SUBLANES - 1:SUBLANES, :], hi[SUBLANES - 1:SUBLANES, :]
        o_s5r[0, :, cs_] = c_re
        o_s5i[0, :, cs_] = c_im

    def s5_out(c0):
        cs_ = slice(c0, c0 + 256)
        y5_parts.append(_dot_nt(s_hre[:, cs_], s5_ct[0, :, cs_]) - _dot_nt(s_him[:, cs_], s5_ct[1, :, cs_]))

    def s5_finish():
        if sample:
            o_s5r[...] = _place_rows(get1[...], s_hre[...])
            o_s5i[...] = _place_rows(get1[...], s_him[...])
        y5 = (y5_parts[0] + y5_parts[1]) + (y5_parts[2] + y5_parts[3]) + s5_d[...] * u
        yy = _dot(y5, glu_w[...])
        ymix[:, 256:512] = masked(_rms(yy[:, 0:256] * _sigmoid(yy[:, 256:512]), s5_nrm[...]))

    for c0 in range(0, S5_WIDTH, 256):
        side_steps.append(functools.partial(s5_scan, c0))
        if not sample:
            side_steps.append(functools.partial(s5_carry, c0))
        side_steps.append(functools.partial(s5_out, c0))
    side_steps.append(s5_finish)

    def side(n=1):
        for _ in range(min(n, len(side_steps))):
            side_steps.pop(0)()
            fill()

    cs = SLOT if sample else GLA_CHUNK
    n_sub, sh = tl // cs, cs.bit_length() - 1
    wide = n_sub * 128
    gla = {}

    def expand_sub(x):
        rr_ = lax.broadcasted_iota(jnp.int32, (tl, wide), 0) >> sh
        cc_ = lax.broadcasted_iota(jnp.int32, (tl, wide), 1) >> 7
        return jnp.where(rr_ == cc_, jnp.concatenate([x] * n_sub, axis=1), 0.0)

    def gla_decays():
        rt = lax.broadcasted_iota(jnp.int32, (tl, tl), 0)
        ct = lax.broadcasted_iota(jnp.int32, (tl, tl), 1)
        sub_same = (rt >> sh) == (ct >> sh)
        m_cum = sub_same & (ct <= rt)
        m_ref = sub_same & ((ct & (cs - 1)) < cs // 2)
        m3 = jnp.concatenate([m_cum.astype(BF16), m_ref.astype(BF16), sub_same.astype(BF16)], axis=0)
        lg = s_lg[...]
        c3 = _dot_exact01(m3, lg)
        cumg, refg, clg = c3[0:tl], c3[tl:2 * tl], c3[2 * tl:3 * tl]
        gla_q = zin[:, Z_Q:Z_Q + 128] * (GLA_DK ** -0.5)
        gla_k = masked(zin[:, Z_K:Z_K + 128])
        gla.update(m_cum=m_cum, lg=lg, v=zin[:, Z_V:Z_V + 256],
                   qe=gla_q * jnp.exp(cumg - refg), ke=gla_k * jnp.exp(refg - cumg),
                   qd=gla_q * jnp.exp(cumg), kt=gla_k * jnp.exp(clg - cumg))

    def gla_intra(h):
        klane = lax.broadcasted_iota(jnp.int32, (1, 128), 1) >> 5
        vlane = lax.broadcasted_iota(jnp.int32, (1, 256), 1) >> 6
        sc = jnp.where(gla["m_cum"], _dot_nt(jnp.where(klane == h, gla["qe"], 0.0), gla["ke"]), 0.0)
        part = jnp.where(vlane == h, _dot(sc, gla["v"]), 0.0)
        gla["o"] = part if h == 0 else gla["o"] + part

    def gla_updates():
        head_diag = (((lax.broadcasted_iota(jnp.int32, (wide, 256), 0) >> 5) & 3)
                     == (lax.broadcasted_iota(jnp.int32, (wide, 256), 1) >> 6))
        msub = ((lax.broadcasted_iota(jnp.int32, (tl, n_sub), 0) >> sh)
                == lax.broadcasted_iota(jnp.int32, (tl, n_sub), 1)).astype(BF16)
        gla.update(head_diag=head_diag,
                   u_all=jnp.where(head_diag, _dot_tn(expand_sub(gla["kt"]), gla["v"]), 0.0),
                   dcol=jnp.exp(_dot_exact01_tn(gla["lg"], msub)))

    def gla_state():
        u_all, dcol = gla["u_all"], gla["dcol"]
        if sample:
            w2 = st_gla[...].reshape(n_sub * 128, GLA_DV)
            e4 = ((lax.broadcasted_iota(jnp.int32, (GLA_DV, 256), 1) & (GLA_DV - 1))
                  == lax.broadcasted_iota(jnp.int32, (GLA_DV, 256), 0)).astype(BF16)
            w_all = jnp.where(gla["head_diag"], _dot_x_exact01(w2, e4), 0.0)
        else:
            ws = []
            w = w_scr[...]
            for i in range(n_sub):
                ws.append(w)
                w = w * dcol[:, i:i + 1] + u_all[i * 128:(i + 1) * 128, :]
            w_scr[...] = w
            w_all = jnp.concatenate(ws, axis=0)
            for h in range(GLA_HEADS):
                o_gla[0, h] = w[h * GLA_DK:(h + 1) * GLA_DK, h * GLA_DV:(h + 1) * GLA_DV]
        s_y[:, 768:1024] = gla["o"] + _dot(expand_sub(gla["qd"]), w_all)
        if sample:
            for i in range(n_sub):
                w = w_all[i * 128:(i + 1) * 128, :] * dcol[:, i:i + 1] + u_all[i * 128:(i + 1) * 128, :]
                for h in range(GLA_HEADS):
                    o_gla[i, h] = w[h * GLA_DK:(h + 1) * GLA_DK, h * GLA_DV:(h + 1) * GLA_DV]

    side_steps += [gla_decays] + [functools.partial(gla_intra, h) for h in range(GLA_HEADS)] + [gla_updates, gla_state]

    c = SCAN_CHUNK
    groups = range(tl // c)
    heads = range(4)
    pairs = [(i, h) for i in groups for h in heads]
    spg = c // SLOT
    r = lax.broadcasted_iota(jnp.int32, (c, c), 0)
    s = lax.broadcasted_iota(jnp.int32, (c, c), 1)
    if sample:
        same = (r >> 3) == (s >> 3)
        lower, strict, upper = (s <= r) & same, (s < r) & same, (r <= s) & same
    else:
        lower, strict, upper = s <= r, s < r, r <= s
    lower01, upper01 = lower.astype(BF16), upper.astype(BF16)
    grp_rows = [pl.ds(i * c, c) for i in groups]

    def expand(x):
        n = x.shape[1]
        rr = lax.broadcasted_iota(jnp.int32, (c, spg * n), 0) >> 3
        cc = lax.broadcasted_iota(jnp.int32, (c, spg * n), 1) >> (n.bit_length() - 1)
        return jnp.where(rr == cc, jnp.concatenate([x] * spg, axis=1), 0.0)

    la = [s_la[rs, :] for rs in grp_rows]
    cum = [_dot_exact01(lower01, x) for x in la]
    side()
    cum_t = [_dot_exact01_tn(x, upper01) for x in la]
    side()
    if sample:
        same01 = same.astype(BF16)
        cl = [_dot_exact01(same01, x) for x in la]
    else:
        cl = [x[c - 1:c, :] for x in cum]
    tail = [jnp.exp(a_ - b_) for a_, b_ in zip(cl, cum)]
    ecum = [jnp.exp(x) for x in cum]
    ecl = [jnp.exp(x) for x in cl]

    def col(x, ln):
        return x[:, ln:ln + 1]

    def dec_of(i, ln):
        return jnp.where(lower, jnp.exp(col(cum[i], ln) - cum_t[i][ln:ln + 1, :]), 0.0)

    xbc = [s_xbc[rs, :] for rs in grp_rows]
    dt = [s_dt[rs, :] for rs in grp_rows]
    bgs = [[xbc[i][:, 256 + g * SSD_STATE:256 + (g + 1) * SSD_STATE] for g in range(2)] for i in groups]
    cgs = [[xbc[i][:, 512 + g * SSD_STATE:512 + (g + 1) * SSD_STATE] for g in range(2)] for i in groups]
    cb = [[_dot_nt(cgs[i][g], bgs[i][g]) for g in range(2)] for i in groups]
    side()
    xh = {(i, h): xbc[i][:, h * SSD_HEADDIM:(h + 1) * SSD_HEADDIM] for i, h in pairs}
    xq = {p: xh[p] * col(dt[p[0]], SM_DT + p[1]) for p in pairs}
    y_ssd = {(i, h): _dot(cb[i][h // 2] * dec_of(i, SM_DT + h), xq[i, h]) + psm[2:3, h:h + 1] * xh[i, h]
             for i, h in pairs}
    side()
    bt = {(i, h): bgs[i][h // 2] * col(tail[i], SM_DT + h) for i, h in pairs}
    cq = {(i, h): cgs[i][h // 2] * col(ecum[i], SM_DT + h) for i, h in pairs}
    upd = {p: _dot_tn(xq[p], expand(bt[p]) if sample else bt[p]) for p in pairs}

    side()
    qkv = [s_qkv[rs, :] for rs in grp_rows]
    beta = [s_beta[rs, :] for rs in grp_rows]
    hd = GDN_HEADDIM
    gq_ = {(i, h): qkv[i][:, h * hd:(h + 1) * hd] for i, h in pairs}
    gk = {(i, h): qkv[i][:, 256 + h * hd:256 + (h + 1) * hd] for i, h in pairs}
    gv = {(i, h): qkv[i][:, 512 + h * hd:512 + (h + 1) * hd] for i, h in pairs}
    gdec = {(i, h): dec_of(i, SM_GA + h) for i, h in pairs}
    gb = {(i, h): col(beta[i], SM_GB + h) for i, h in pairs}
    kk = {p: _dot_nt(gk[p], gk[p]) for p in pairs}
    side()
    qk_d = {p: _dot_nt(gq_[p], gk[p]) * gdec[p] for p in pairs}
    side()
    a_mat = {p: jnp.where(strict, gdec[p] * kk[p], 0.0) * gb[p] for p in pairs}
    t_inv = dict(zip(pairs, _unit_lower_inverse([a_mat[p] for p in pairs], c,
                                                blk=SLOT if sample else 2 * SUBLANES, merge=not sample,
                                                between=side)))
    ge = {(i, h): col(ecum[i], SM_GA + h) for i, h in pairs}
    rhs = {p: jnp.concatenate([gv[p] * gb[p], gk[p] * (gb[p] * ge[p])], axis=1) for p in pairs}
    x = {p: _dot(t_inv[p], rhs[p]) for p in pairs}
    side()
    ktl = {(i, h): gk[i, h] * col(tail[i], SM_GA + h) for i, h in pairs}
    gqe = {p: gq_[p] * ge[p] for p in pairs}

    if sample:
        slots = [pl.ds(i * spg, spg) for i in groups]
        last = [ecl[i].reshape(spg, SLOT, 128)[:, SLOT - 1:SLOT, :] for i in groups]
        s_all = {(i, h): st_gdn[slots[i], h] for i, h in pairs}
        m = {p: jnp.concatenate([expand(x[p][:, hd:2 * hd]), expand(gqe[p])], axis=0) for p in pairs}
        rr = {p: _dot(m[p], s_all[p].reshape(spg * hd, hd)) for p in pairs}
        delta = {p: x[p][:, 0:hd] - rr[p][0:c] for p in pairs}
        og = {p: rr[p][c:2 * c] + _dot(qk_d[p], delta[p]) for p in pairs}
        un = {p: _dot_tn(expand(ktl[p]), delta[p]) for p in pairs}
        h_all = {(i, h): [st_ssd[i * spg + g, h] for g in range(spg)] for i, h in pairs}
        yi = {p: _dot_nt(expand(cq[p]), jnp.concatenate(h_all[p], axis=1)) for p in pairs}
        for i, h in pairs:
            rs = grp_rows[i]
            o_gdn[slots[i], h] = (s_all[i, h] * last[i][:, :, SM_GA + h:SM_GA + h + 1]
                                  + un[i, h].reshape(spg, hd, hd))
            s_y[rs, 512 + h * hd:512 + (h + 1) * hd] = og[i, h]
            for g in range(spg):
                o_ssd[i * spg + g, h] = (h_all[i, h][g] * last[i][g, :, SM_DT + h:SM_DT + h + 1]
                                         + upd[i, h][:, g * SSD_STATE:(g + 1) * SSD_STATE])
            s_y[rs, h * SSD_HEADDIM:(h + 1) * SSD_HEADDIM] = y_ssd[i, h] + yi[i, h]
    else:
        fg = {p: _dot_tn(ktl[p], x[p]) for p in pairs}
        side()
        qo = {p: _dot(qk_d[p], x[p]) for p in pairs}
        side()
        for i in groups:
            rs = grp_rows[i]
            s0 = [o_gdn[0, h] for h in heads]
            h0 = [o_ssd[0, h] for h in heads]
            m = [jnp.concatenate([gqe[i, h] - qo[i, h][:, hd:2 * hd], fg[i, h][:, hd:2 * hd]], axis=0) for h in heads]
            rr = [_dot(m[h], s0[h]) for h in heads]
            yi = [_dot_nt(cq[i, h], h0[h]) for h in heads]
            for h in heads:
                o_gdn[0, h] = s0[h] * col(ecl[i], SM_GA + h) - rr[h][c:c + hd] + fg[i, h][:, 0:hd]
                s_y[rs, 512 + h * hd:512 + (h + 1) * hd] = rr[h][0:c] + qo[i, h][:, 0:hd]
                o_ssd[0, h] = h0[h] * col(ecl[i], SM_DT + h) + upd[i, h]
                s_y[rs, h * SSD_HEADDIM:(h + 1) * SSD_HEADDIM] = y_ssd[i, h] + yi[h]
            side()

    side(len(side_steps))

    fill(len(pending))

    e256 = e01[0:256, 0:256]
    y = s_y[:, 0:256] * _silu(zin[:, Z_Z:Z_Z + 256])
    ymix[:, 0:256] = masked(_rms(y, ssd_nrm[...]))
    o = s_y[:, 512:768]
    o = o * lax.rsqrt(_head_meansq(o, e256, GDN_HEADDIM) + EPS) * gdn_nrm[...]
    ymix[:, 512:768] = masked(o * _silu(zin[:, Z_GZ:Z_GZ + 256]))
    o = s_y[:, 768:1024]
    o = o * lax.rsqrt(_head_meansq(o, e256, GLA_DV) + EPS) * gla_nrm[...]
    ymix[:, 768:1024] = masked(o * _silu(zin[:, Z_GG:Z_GG + 256]))


def _slot_move_matrices(tl, rows):
    slots = tl // SLOT
    put = np.zeros((tl, rows * slots), np.float32)
    get = np.zeros((rows * slots, tl), np.float32)
    for g in range(slots):
        for j in range(rows):
            put[g * SLOT + SLOT_OFF - rows + j, j * slots + g] = 1.0
            get[j * slots + g, g * SLOT + SLOT - rows + j] = 1.0
    return jnp.asarray(put, BF16), jnp.asarray(get, BF16)


def _mixers(x2d, mod, l, prm, *, batch, rows_per_seq, sample, states=None, carry=None):
    t = x2d.shape[0]
    tl = min(SAMPLE_MIX_TILE if sample else TILE_TOKENS, t)
    n_tiles = t // tl
    nseq = t // SLOT if sample else batch
    nrow = CONV_W - 1
    dims = [(SSD_HEADS, SSD_HEADDIM, SSD_STATE), (GDN_HEADS, GDN_HEADDIM, GDN_HEADDIM), (GLA_HEADS, GLA_DK, GLA_DV)]
    grid = (n_tiles + 1,)
    nxt = lambda g: jnp.minimum(g, n_tiles - 1)
    cur = lambda g: jnp.maximum(g - 1, 0)
    tok = lambda w: pl.BlockSpec((tl, w), lambda g: (cur(g), 0))
    if sample:
        slots = tl // SLOT
        tiles = 1
        st_spec = lambda d: pl.BlockSpec((None, slots) + d, lambda g: (l, cur(g), 0, 0, 0))
        st_shape = lambda d: jax.ShapeDtypeStruct((DEPTH, nseq) + d, F32)
        cv_in = lambda w: pl.BlockSpec((None, nrow, slots, w), lambda g: (l, 0, cur(g), 0))
        row_in = lambda w: pl.BlockSpec((None, slots, w), lambda g: (l, cur(g), 0))
        cv_spec = pl.BlockSpec((nrow, slots, 768), lambda g: (0, cur(g), 0))
        cv_shape = jax.ShapeDtypeStruct((nrow, nseq, 768), F32)
        s5_spec = pl.BlockSpec((slots, S5_WIDTH), lambda g: (cur(g), 0))
        s5_shape = jax.ShapeDtypeStruct((nseq, S5_WIDTH), F32)
        mspecs = [pl.BlockSpec((None, slots, D_MODEL), (lambda g, j=j: (l, nxt(g), j))) for j in (0, 1)]
        mops = [mod] * 2
    else:
        tiles = rows_per_seq // tl
        seq_of = lambda g: cur(g) // tiles
        st_spec = lambda d: pl.BlockSpec((1,) + d, lambda g: (seq_of(g), 0, 0, 0))
        st_shape = lambda d: jax.ShapeDtypeStruct((batch,) + d, F32)
        cv_spec = pl.BlockSpec((1, nrow, 768), lambda g: (seq_of(g), 0, 0))
        cv_shape = jax.ShapeDtypeStruct((batch, nrow, 768), F32)
        s5_spec = pl.BlockSpec((1, 1, S5_WIDTH), lambda g: (seq_of(g), 0, 0))
        s5_shape = jax.ShapeDtypeStruct((batch, 1, S5_WIDTH), F32)
        m3 = mod.reshape(DEPTH * batch * 6, 1, D_MODEL)
        mspecs = [pl.BlockSpec((1, 1, D_MODEL), (lambda g, j=j: ((l * batch + nxt(g) // tiles) * 6 + j, 0, 0)))
                  for j in (0, 1)]
        mops = [m3] * 2
    names = ["psmall", "ssd_cw", "ssd_cb", "ssd_nrm", "s5_b", "s5_ct", "s5_l1", "s5_ps" if sample else "s5_pp",
             "s5_d", "glu_w", "s5_nrm", "gdn_cw", "gdn_nrm", "gla_w2", "gla_bg", "gla_nrm"]
    consts = [prm["norm_mix"], prm["w_in"]] + [prm[n] for n in names]
    in_specs = ([pl.BlockSpec((tl, D_MODEL), lambda g: (nxt(g), 0))] + mspecs
                + [_layer_spec(a, l, 1) for a in consts] + [_shared_spec(prm["e01"])])
    operands = [x2d] + mops + consts + [prm["e01"]]
    aliases = {}
    if sample:
        put3, get3 = _slot_move_matrices(tl, nrow)
        _, get1 = _slot_move_matrices(tl, 1)
        in_specs += [st_spec(dims[0]), cv_in(SSD_CONV_DIM), row_in(S5_WIDTH), row_in(S5_WIDTH),
                     st_spec(dims[1]), cv_in(GDN_CONV_DIM), st_spec(dims[2]),
                     _shared_spec(put3), _shared_spec(get3), _shared_spec(get1)]
        operands += list(states) + [put3, get3, get1]
        if carry is not None:
            aliases = {len(operands) + k: out_idx for k, out_idx in enumerate((1, 5, 7))}
            in_specs += [pl.BlockSpec(memory_space=pl.ANY)] * 3
            operands += list(carry)
    out_specs = [tok(D_MODEL), st_spec(dims[0]), cv_spec, s5_spec, s5_spec, st_spec(dims[1]), cv_spec,
                 st_spec(dims[2])]
    out_shape = [jax.ShapeDtypeStruct((t, D_MODEL), F32), st_shape(dims[0]), cv_shape, s5_shape, s5_shape,
                 st_shape(dims[1]), cv_shape, st_shape(dims[2])]
    scratch = [pltpu.VMEM((tl + SUBLANES, 768), F32), pltpu.VMEM((tl + SUBLANES, 768), F32),
               pltpu.VMEM((tl, 768), F32), pltpu.VMEM((tl, 128), F32), pltpu.VMEM((tl, 128), F32),
               pltpu.VMEM((tl, 128), F32), pltpu.VMEM((tl, 768), F32), pltpu.VMEM((tl, 128), F32),
               pltpu.VMEM((tl, D_MODEL), F32), pltpu.VMEM((tl, S5_WIDTH), F32), pltpu.VMEM((tl, S5_WIDTH), F32),
               pltpu.VMEM((GLA_HEADS * GLA_DK, GLA_HEADS * GLA_DV), F32), pltpu.VMEM((tl, NZ), F32),
               pltpu.VMEM((tl, NZ), F32)]
    return pl.pallas_call(
        functools.partial(_mixers_kernel, tl=tl, sample=sample, n_carry=len(aliases), tiles_per_seq=tiles),
        grid=grid, in_specs=in_specs, out_specs=out_specs, out_shape=out_shape, scratch_shapes=scratch,
        input_output_aliases=aliases,
        compiler_params=pltpu.CompilerParams(dimension_semantics=("arbitrary",) * len(grid),
                                             vmem_limit_bytes=VMEM_LIMIT),
        name="mixers_sample" if sample else "mixers_prompt",
    )(*operands)


def _outffn_kernel(*refs, tm, sample, final):
    x_ref, y_ref, g1_ref, sh_ref, sc_ref, g2_ref, nf_ref, wout, up, cw, cb, down = refs[:12]
    k = 12
    hist = put2 = get2 = fin = None
    if sample:
        hist, put2, get2 = refs[k:k + 3]
        k += 3
    if final:
        fin = refs[k]
        k += 1
    x2_ref, cv_ref, ext = refs[k], refs[k + 1], refs[k + 2]

    if sample:
        ex = lambda r: _expand_rows(r[...], SLOT)
        g1, sh, sc, g2 = ex(g1_ref), ex(sh_ref), ex(sc_ref), ex(g2_ref)
        valid = (lax.broadcasted_iota(jnp.int32, (tm, 1), 0) & (SLOT - 1)) >= SLOT_OFF
        ext[0:SUBLANES, :] = jnp.zeros((SUBLANES, 2 * D_FF), F32)
    else:
        g1, sh, sc, g2 = g1_ref[0], sh_ref[0], sc_ref[0], g2_ref[0]

        @pl.when(pl.program_id(1) == 0)
        def _():
            ext[0:SUBLANES, :] = jnp.zeros((SUBLANES, 2 * D_FF), F32)

    x1 = x_ref[...] + g1 * _dot(y_ref[...], wout[...])
    h = (_rms(x1, nf_ref[...]) * (1.0 + sc) + sh).astype(BF16)
    for j in range(0, 2 * D_FF, FF_CHUNK):
        uj = jnp.dot(h, up[:, j:j + FF_CHUNK], preferred_element_type=F32)
        if sample:
            hj = jnp.concatenate([hist[:, r, j:j + FF_CHUNK] for r in range(FFN_CONV_W - 1)], axis=0)
            uj = jnp.where(valid, uj, _place_rows(put2[...], hj))
        ext[pl.ds(SUBLANES, tm), j:j + FF_CHUNK] = uj

    def conv(col):
        acc = cb[:, col:col + FF_CHUNK]
        for kk in range(FFN_CONV_W):
            acc = acc + cw[kk:kk + 1, col:col + FF_CHUNK] * ext[pl.ds(SUBLANES - (FFN_CONV_W - 1) + kk, tm),
                                                                 col:col + FF_CHUNK]
        return acc

    f = jnp.zeros((tm, D_MODEL), F32)
    for j in range(0, D_FF, FF_CHUNK):
        act = _silu(conv(j)) * conv(D_FF + j)
        f = f + _dot(act, down[j:j + FF_CHUNK, :])
    x2 = x1 + g2 * f
    if final:
        x2 = _rms(x2, fin[...])
    if sample:
        x2_ref[...] = jnp.where(valid, x2, 0.0)
        for j in range(0, 2 * D_FF, FF_CHUNK):
            got = _place_rows(get2[...], ext[pl.ds(SUBLANES, tm), j:j + FF_CHUNK])
            for r in range(FFN_CONV_W - 1):
                cv_ref[:, r, j:j + FF_CHUNK] = got[r * (tm // SLOT):(r + 1) * (tm // SLOT)]
    else:
        x2_ref[...] = x2
        tail = ext[pl.ds(tm + SUBLANES - (FFN_CONV_W - 1), FFN_CONV_W - 1), :]
        cv_ref[0] = tail
        ext[pl.ds(SUBLANES - (FFN_CONV_W - 1), FFN_CONV_W - 1), :] = tail


def _out_ffn(x2d, ymix, mod, l, prm, *, batch, rows_per_seq, sample, hist=None, final_gain=None):
    t = x2d.shape[0]
    tm = min(SAMPLE_FFN_TILE if sample else TILE_TOKENS, t)
    final = final_gain is not None
    nrows = FFN_CONV_W - 1
    if sample:
        slots = tm // SLOT
        grid = (t // tm,)
        tok = lambda w: pl.BlockSpec((tm, w), lambda i: (i, 0))
        mspecs = [pl.BlockSpec((None, slots, D_MODEL), (lambda i, j=j: (l, i, j))) for j in (2, 3, 4, 5)]
        mops = [mod] * 4
        cv_spec = pl.BlockSpec((slots, nrows, 2 * D_FF), lambda i: (i, 0, 0))
        cv_shape = jax.ShapeDtypeStruct((t // SLOT, nrows, 2 * D_FF), F32)
    else:
        tiles = rows_per_seq // tm
        grid = (batch, tiles)
        tok = lambda w: pl.BlockSpec((tm, w), lambda b, j: (b * tiles + j, 0))
        m3 = mod.reshape(DEPTH * batch * 6, 1, D_MODEL)
        mspecs = [pl.BlockSpec((1, 1, D_MODEL), (lambda b, i, j=j: ((l * batch + b) * 6 + j, 0, 0))) for j in (2, 3, 4, 5)]
        mops = [m3] * 4
        cv_spec = pl.BlockSpec((1, nrows, 2 * D_FF), lambda b, j: (b, 0, 0))
        cv_shape = jax.ShapeDtypeStruct((batch, nrows, 2 * D_FF), F32)
    consts = [prm[n] for n in ("norm_ffn", "w_out", "ffn_up", "ffn_cw", "ffn_cb", "ffn_down")]
    in_specs = [tok(D_MODEL), tok(D_MODEL)] + mspecs + [_layer_spec(a, l, len(grid)) for a in consts]
    operands = [x2d, ymix] + mops + consts
    if sample:
        put2, get2 = _slot_move_matrices(tm, nrows)
        in_specs += [pl.BlockSpec((None, slots, nrows, 2 * D_FF), lambda i: (l, i, 0, 0)),
                     _shared_spec(put2), _shared_spec(get2)]
        operands += [hist, put2, get2]
    if final:
        in_specs.append(_shared_spec(final_gain))
        operands.append(final_gain)
    return pl.pallas_call(
        functools.partial(_outffn_kernel, tm=tm, sample=sample, final=final),
        grid=grid, in_specs=in_specs,
        out_specs=[tok(D_MODEL), cv_spec],
        out_shape=[jax.ShapeDtypeStruct((t, D_MODEL), F32), cv_shape],
        scratch_shapes=[pltpu.VMEM((tm + SUBLANES, 2 * D_FF), F32)],
        compiler_params=pltpu.CompilerParams(dimension_semantics=("arbitrary",) * len(grid),
                                             vmem_limit_bytes=VMEM_LIMIT),
        name="out_ffn_sample" if sample else "out_ffn_prompt",
    )(*operands)


def _reorder_w_in(w_in):
    o = np.cumsum([0, 256, 768, 4, 256, 768, 256, 4, 4, 128, 128, 256, 256, 16])
    seg = lambda i: w_in[:, :, o[i]:o[i + 1]]
    small = [seg(2), seg(6), seg(7), seg(12)]
    pad = jnp.zeros(w_in.shape[:2] + (128 - 28,), w_in.dtype)
    cols = [seg(1), seg(4), seg(0), seg(3), seg(5), seg(10), seg(11), seg(8), seg(9)] + small + [pad]
    return jnp.concatenate(cols, axis=-1).astype(BF16)


def _pad_lanes(a, lane0, width=128):
    return jnp.pad(a, ((0, 0), (lane0, width - lane0 - a.shape[1])))[:, None, :]


def kernel(x_prompt, x_sample, c_prompt, c_sample, state_ssd, state_ssd_conv, state_s5_re, state_s5_im,
           state_gdn, state_gdn_conv, state_gla, state_ffn_conv, ada_w, ada_b, norm_mix, norm_ffn, w_in, w_out,
           ssd_conv_w, ssd_conv_b, ssd_dt_bias, ssd_a_log, ssd_d, ssd_norm, s5_a_re, s5_a_im, s5_log_dt,
           s5_b_re, s5_b_im, s5_c_re, s5_c_im, s5_d, s5_glu_w, s5_norm, gdn_conv_w, gdn_a_log, gdn_dt_bias,
           gdn_norm, gla_wg2, gla_bg, gla_norm, ffn_up, ffn_conv_w, ffn_conv_b, ffn_down, final_norm):
    nb, seq = x_prompt.shape[0], x_prompt.shape[1]
    ns, dseq = x_sample.shape[0], x_sample.shape[1]
    assert dseq == SLOT - SLOT_OFF

    mod_p, mod_s = _modulation(c_prompt.astype(F32), c_sample.astype(F32), ada_w, ada_b)
    s5_b, s5_ct, s5_l1, s5_pp, s5_ps = _s5_prepare(s5_a_re, s5_a_im, s5_log_dt, s5_b_re, s5_b_im, s5_c_re, s5_c_im)

    row = lambda a: a[:, None, :]
    psmall = jnp.concatenate([
        _pad_lanes(ssd_dt_bias, SM_DT) + _pad_lanes(gdn_dt_bias, SM_GA),
        _pad_lanes(ssd_a_log, SM_DT) + _pad_lanes(gdn_a_log, SM_GA),
        _pad_lanes(ssd_d, 0), jnp.zeros((DEPTH, SUBLANES - 3, 128), F32)], axis=1)
    prm = dict(
        psmall=psmall, ssd_cw=ssd_conv_w, ssd_cb=row(ssd_conv_b), ssd_nrm=row(ssd_norm),
        s5_b=s5_b, s5_ct=s5_ct, s5_l1=s5_l1, s5_pp=s5_pp, s5_ps=s5_ps, s5_d=row(s5_d),
        glu_w=s5_glu_w.astype(BF16), s5_nrm=row(s5_norm), gdn_cw=gdn_conv_w,
        gdn_nrm=row(jnp.tile(gdn_norm, (1, GDN_HEADS))),
        gla_w2=jnp.pad(gla_wg2, ((0, 0), (SM_LR, 128 - SM_LR - GLA_GATE_RANK), (0, 0))).astype(BF16),
        gla_bg=row(gla_bg), gla_nrm=row(jnp.tile(gla_norm, (1, GLA_HEADS))),
        e01=jnp.asarray(np.arange(512)[:, None] // 64 == np.arange(512)[None, :] // 64, BF16),
        norm_mix=row(norm_mix), norm_ffn=row(norm_ffn), w_in=_reorder_w_in(w_in), w_out=w_out.astype(BF16),
        ffn_up=ffn_up.astype(BF16), ffn_cw=ffn_conv_w, ffn_cb=row(ffn_conv_b), ffn_down=ffn_down.astype(BF16))
    fin = final_norm.reshape(1, D_MODEL)

    xp = x_prompt.astype(F32).reshape(nb * seq, D_MODEL)
    p_new = []
    for l in range(DEPTH):
        ymix, h_ssd, cv_ssd, h5r, h5i, s_gdn, cv_gdn, s_gla = _mixers(xp, mod_p, l, prm, batch=nb, rows_per_seq=seq,
                                                                      sample=False)
        xp, cv_ffn = _out_ffn(xp, ymix, mod_p, l, prm, batch=nb, rows_per_seq=seq, sample=False,
                              final_gain=fin if l == DEPTH - 1 else None)
        p_new.append((jnp.swapaxes(h_ssd, 2, 3), cv_ssd, h5r.reshape(nb, S5_GROUPS, S5_STATE),
                      h5i.reshape(nb, S5_GROUPS, S5_STATE), s_gdn, cv_gdn, s_gla, cv_ffn))

    xs = jnp.pad(x_sample.astype(F32), ((0, 0), (SLOT_OFF, 0), (0, 0))).reshape(ns * SLOT, D_MODEL)
    f32 = lambda a: a.astype(F32)
    states = (jnp.swapaxes(f32(state_ssd), 3, 4), jnp.swapaxes(f32(state_ssd_conv), 1, 2),
              f32(state_s5_re).reshape(DEPTH, ns, S5_WIDTH), f32(state_s5_im).reshape(DEPTH, ns, S5_WIDTH),
              f32(state_gdn), jnp.swapaxes(f32(state_gdn_conv), 1, 2), f32(state_gla))
    hist = f32(state_ffn_conv)
    s_new, carry = [], None
    for l in range(DEPTH):
        ymix, h_ssd, cv_ssd, h5r, h5i, s_gdn, cv_gdn, s_gla = _mixers(xs, mod_s, l, prm, batch=ns, rows_per_seq=SLOT,
                                                                      sample=True, states=states, carry=carry)
        carry = (h_ssd, s_gdn, s_gla)
        xs, cv_ffn = _out_ffn(xs, ymix, mod_s, l, prm, batch=ns, rows_per_seq=SLOT, sample=True, hist=hist,
                              final_gain=fin if l == DEPTH - 1 else None)
        s_new.append((jnp.swapaxes(cv_ssd, 0, 1), h5r.reshape(ns, S5_GROUPS, S5_STATE),
                      h5i.reshape(ns, S5_GROUPS, S5_STATE), jnp.swapaxes(cv_gdn, 0, 1), cv_ffn))

    y_p = xp.reshape(nb, seq, D_MODEL).astype(x_prompt.dtype)
    y_s = xs.reshape(ns, SLOT, D_MODEL)[:, SLOT_OFF:, :].astype(x_sample.dtype)
    p_st = [jnp.stack(t) for t in zip(*p_new)]
    s_cv_ssd, s_5r, s_5i, s_cv_gdn, s_cv_ffn = [jnp.stack(t) for t in zip(*s_new)]
    s_ssd, s_gdn, s_gla = carry
    return (y_p, y_s, *p_st,
            jnp.swapaxes(s_ssd, 3, 4), s_cv_ssd, s_5r, s_5i, s_gdn, s_cv_gdn, s_gla, s_cv_ffn)
```

```python
import functools

import numpy as np
import jax
import jax.numpy as jnp
from jax import lax
from jax.experimental import pallas as pl
from jax.experimental.pallas import tpu as pltpu

F32 = jnp.float32
BF16 = jnp.bfloat16

D_MODEL = 1024
DEPTH = 2
GROUP_WIDTH = 256
CONV_W = 4
SSD_HEADS = 4
SSD_HEADDIM = 64
SSD_STATE = 128
SSD_CONV_DIM = 768
S5_GROUPS = 16
S5_GROUP_CH = 16
S5_STATE = 64
S5_WIDTH = S5_GROUPS * S5_STATE
GDN_HEADS = 4
GDN_HEADDIM = 64
GDN_CONV_DIM = 768
GLA_HEADS = 4
GLA_DK = 32
GLA_DV = 64
GLA_GATE_RANK = 16
GLA_TAU = 16.0
D_FF = 2816
FFN_CONV_W = 3
EPS = 1e-6

NZ = 3200
Z_XBC, Z_QKV, Z_Z, Z_U, Z_GZ, Z_V, Z_GG, Z_Q, Z_K, Z_SM = 0, 768, 1536, 1792, 2048, 2304, 2560, 2816, 2944, 3072
SM_DT, SM_GA, SM_GB, SM_LR = 0, 4, 8, 12

SUBLANES = 8
SLOT = SUBLANES
SLOT_OFF = SLOT - 4
SCAN_CHUNK = 64
GLA_CHUNK = 16
TILE_TOKENS = 256
SAMPLE_MIX_TILE = 128
FFN_TILE = 512
SAMPLE_FFN_TILE = 128
FF_CHUNK = 256
PROJ_COLS = 256
VMEM_LIMIT = 56 * 1024 * 1024
STATE_DIMS = ((SSD_HEADS, SSD_HEADDIM, SSD_STATE), (GDN_HEADS, GDN_HEADDIM, GDN_HEADDIM), (GLA_HEADS, GLA_DK, GLA_DV))


def _silu(x):
    return x * (1.0 / (1.0 + jnp.exp(-x)))


def _sigmoid(x):
    return 1.0 / (1.0 + jnp.exp(-x))


def _softplus(x):
    return jnp.maximum(x, 0.0) + jnp.log(1.0 + jnp.exp(-jnp.abs(x)))


def _dot(a, b):
    return jnp.dot(a.astype(BF16), b.astype(BF16), preferred_element_type=F32)


def _dot_nt(a, b):
    return lax.dot_general(a.astype(BF16), b.astype(BF16), (((1,), (1,)), ((), ())), preferred_element_type=F32)


def _dot_tn(a, b):
    return lax.dot_general(a.astype(BF16), b.astype(BF16), (((0,), (0,)), ((), ())), preferred_element_type=F32)


def _split_hi_lo(x):
    hi = x.astype(BF16)
    lo = (x - hi.astype(F32)).astype(BF16)
    return hi, lo


def _dot_exact01(m01, x):
    hi, lo = _split_hi_lo(x)
    return jnp.dot(m01, hi, preferred_element_type=F32) + jnp.dot(m01, lo, preferred_element_type=F32)


def _dot_exact01_tn(x, m01):
    hi, lo = _split_hi_lo(x)
    dn = (((0,), (0,)), ((), ()))
    return (lax.dot_general(hi, m01, dn, preferred_element_type=F32)
            + lax.dot_general(lo, m01, dn, preferred_element_type=F32))


def _place_rows(p01, x):
    hi = x.astype(BF16)
    rest = x - hi.astype(F32)
    mid = rest.astype(BF16)
    lo = (rest - mid.astype(F32)).astype(BF16)
    mm = lambda v: jnp.dot(p01, v, preferred_element_type=F32)
    return (mm(hi) + mm(mid)) + mm(lo)


def _dot_x_exact01(x, m01):
    hi, lo = _split_hi_lo(x)
    return jnp.dot(hi, m01, preferred_element_type=F32) + jnp.dot(lo, m01, preferred_element_type=F32)


def _unit_lower_inverse(mats, c, blk, merge, between):
    r = lax.broadcasted_iota(jnp.int32, (c, c), 0)
    s = lax.broadcasted_iota(jnp.int32, (c, c), 1)
    sh = blk.bit_length() - 1
    diag = (r >> sh) == (s >> sh)
    eye = jnp.where(r == s, 1.0, 0.0)
    p = [jnp.where(diag, a, 0.0) for a in mats]
    t = [eye - d for d in p]
    pw = 2
    while pw < blk:
        p = [_dot(d, d) for d in p]
        between()
        t = [m + _dot(m, d) for m, d in zip(t, p)]
        between()
        pw *= 2
    size = blk
    while merge and size < c:
        sh = size.bit_length() - 1
        pair = ((r >> (sh + 1)) == (s >> (sh + 1))) & ((r >> sh) != (s >> sh))
        lt = [_dot(jnp.where(pair, a, 0.0), m) for a, m in zip(mats, t)]
        between()
        t = [m - _dot(m, y) for m, y in zip(t, lt)]
        between()
        size *= 2
    return t


def _shift_rows(x, prev8, d):
    r = pltpu.roll(x, d, 0)
    p = pltpu.roll(prev8, d, 0)
    row = lax.broadcasted_iota(jnp.int32, (SUBLANES, 1), 0)
    return jnp.concatenate([jnp.where(row < d, p, r[0:SUBLANES]), r[SUBLANES:]], axis=0)


def _rms(x, gain):
    return x * lax.rsqrt(jnp.mean(x * x, axis=-1, keepdims=True) + EPS) * gain


def _expand_rows(m, reps):
    g, n = m.shape
    return jnp.broadcast_to(m[:, None, :], (g, reps, n)).reshape(g * reps, n)


def _mod_kernel(cp_ref, cs_ref, w_ref, b_ref, op_ref, os_ref):
    w = w_ref[0].astype(BF16)
    op_ref[0] = _dot(_silu(cp_ref[...]), w) + b_ref[0]
    os_ref[0] = _dot(_silu(cs_ref[...]), w) + b_ref[0]


def _modulation(c_prompt, c_sample, ada_w, ada_b):
    nb, ns = c_prompt.shape[0], c_sample.shape[0]
    tn = 1024
    return pl.pallas_call(
        _mod_kernel,
        grid=(DEPTH, 6 * D_MODEL // tn),
        in_specs=[pl.BlockSpec((nb, D_MODEL), lambda l, j: (0, 0)),
                  pl.BlockSpec((ns, D_MODEL), lambda l, j: (0, 0)),
                  pl.BlockSpec((1, D_MODEL, tn), lambda l, j: (l, 0, j)),
                  pl.BlockSpec((1, 1, tn), lambda l, j: (l, 0, j))],
        out_specs=[pl.BlockSpec((1, nb, tn), lambda l, j: (l, 0, j)),
                   pl.BlockSpec((1, ns, tn), lambda l, j: (l, 0, j))],
        out_shape=[jax.ShapeDtypeStruct((DEPTH, nb, 6 * D_MODEL), F32),
                   jax.ShapeDtypeStruct((DEPTH, ns, 6 * D_MODEL), F32)],
        compiler_params=pltpu.CompilerParams(dimension_semantics=("arbitrary", "arbitrary"),
                                             vmem_limit_bytes=VMEM_LIMIT),
        name="adaln_mod",
    )(c_prompt, c_sample, ada_w, ada_b.reshape(DEPTH, 1, 6 * D_MODEL))


def _s5_prep_kernel(are_ref, aim_ref, ldt_ref, btr_ref, bti_ref, cr_ref, ci_ref, arow_ref, e_ref, gm_ref,
                    b_out, c_out, l1_out, pp_out, ps_out):
    a_re, a_im = are_ref[0], aim_ref[0]
    dt = jnp.exp(ldt_ref[0])
    mag = jnp.exp(dt * a_re)
    ab_re, ab_im = mag * jnp.cos(dt * a_im), mag * jnp.sin(dt * a_im)
    den = a_re * a_re + a_im * a_im
    f_re = ((ab_re - 1.0) * a_re + ab_im * a_im) / den
    f_im = (ab_im * a_re - (ab_re - 1.0) * a_im) / den
    bb_re = f_re * btr_ref[0] - f_im * bti_ref[0]
    bb_im = f_re * bti_ref[0] + f_im * btr_ref[0]
    e01, gm = e_ref[...], gm_ref[...]
    b_out[0, 0] = (_dot_x_exact01(bb_re, e01) * gm).astype(BF16)
    b_out[0, 1] = (_dot_x_exact01(bb_im, e01) * gm).astype(BF16)
    c_out[0, 0] = (_dot_x_exact01(cr_ref[0], e01) * gm).astype(BF16)
    c_out[0, 1] = (_dot_x_exact01(ci_ref[0], e01) * gm).astype(BF16)
    ar, ai, dtr = arow_ref[0, 0:1, :], arow_ref[0, 1:2, :], jnp.exp(arow_ref[0, 2:3, :])
    row = lax.broadcasted_iota(jnp.int32, (SUBLANES, 1), 0)

    def power(k):
        m = jnp.exp(k * dtr * ar)
        return m * jnp.cos(k * dtr * ai), m * jnp.sin(k * dtr * ai)

    for i, d in enumerate((1, 2, 4)):
        pr, pi = power(jnp.full((SUBLANES, 1), float(d), F32))
        keep = row >= d
        l1_out[0, i] = jnp.where(keep, pr, 0.0)
        l1_out[0, 3 + i] = jnp.where(keep, pi, 0.0)
    pr, pi = power((row + 1).astype(F32))
    pp_out[0, 0], pp_out[0, 1] = pr, pi
    pr, pi = power(jnp.maximum(row - (SLOT_OFF - 1), 0).astype(F32))
    ps_out[0, 0] = jnp.where(row >= SLOT_OFF, pr, 0.0)
    ps_out[0, 1] = jnp.where(row >= SLOT_OFF, pi, 0.0)


def _s5_prepare(s5_a_re, s5_a_im, s5_log_dt, s5_b_re, s5_b_im, s5_c_re, s5_c_im):
    rows = S5_GROUPS * S5_GROUP_CH
    rep = lambda a: jnp.repeat(a, S5_GROUP_CH, axis=1)
    ldt = jnp.broadcast_to(s5_log_dt[:, :, None], (DEPTH, S5_GROUPS, S5_STATE))
    bt = lambda b: jnp.swapaxes(b, 2, 3).reshape(DEPTH, rows, S5_STATE)
    arow = jnp.stack([s5_a_re.reshape(DEPTH, S5_WIDTH), s5_a_im.reshape(DEPTH, S5_WIDTH),
                      ldt.reshape(DEPTH, S5_WIDTH)], axis=1)
    arow = jnp.pad(arow, ((0, 0), (0, SUBLANES - 3), (0, 0)))
    e01 = np.zeros((S5_STATE, S5_WIDTH), np.float32)
    e01[np.arange(S5_WIDTH) % S5_STATE, np.arange(S5_WIDTH)] = 1.0
    gm = (np.arange(rows)[:, None] // S5_GROUP_CH == np.arange(S5_WIDTH)[None, :] // S5_STATE).astype(np.float32)
    p3 = lambda: pl.BlockSpec((1, rows, S5_STATE), lambda l: (l, 0, 0))
    tab = lambda n: pl.BlockSpec((1, n, SUBLANES, S5_WIDTH), lambda l: (l, 0, 0, 0))
    return pl.pallas_call(
        _s5_prep_kernel,
        grid=(DEPTH,),
        in_specs=[p3(), p3(), p3(), p3(), p3(), p3(), p3(),
                  pl.BlockSpec((1, SUBLANES, S5_WIDTH), lambda l: (l, 0, 0)),
                  pl.BlockSpec((S5_STATE, S5_WIDTH), lambda l: (0, 0)),
                  pl.BlockSpec((rows, S5_WIDTH), lambda l: (0, 0))],
        out_specs=[pl.BlockSpec((1, 2, rows, S5_WIDTH), lambda l: (l, 0, 0, 0)),
                   pl.BlockSpec((1, 2, rows, S5_WIDTH), lambda l: (l, 0, 0, 0)),
                   tab(6), tab(2), tab(2)],
        out_shape=[jax.ShapeDtypeStruct((DEPTH, 2, rows, S5_WIDTH), BF16),
                   jax.ShapeDtypeStruct((DEPTH, 2, rows, S5_WIDTH), BF16),
                   jax.ShapeDtypeStruct((DEPTH, 6, SUBLANES, S5_WIDTH), F32),
                   jax.ShapeDtypeStruct((DEPTH, 2, SUBLANES, S5_WIDTH), F32),
                   jax.ShapeDtypeStruct((DEPTH, 2, SUBLANES, S5_WIDTH), F32)],
        compiler_params=pltpu.CompilerParams(dimension_semantics=("arbitrary",), vmem_limit_bytes=VMEM_LIMIT),
        name="s5_prepare",
    )(rep(s5_a_re), rep(s5_a_im), rep(ldt), bt(s5_b_re), bt(s5_b_im),
      s5_c_re.reshape(DEPTH, rows, S5_STATE), s5_c_im.reshape(DEPTH, rows, S5_STATE),
      arow, jnp.asarray(e01, BF16), jnp.asarray(gm, F32))


def _layer_spec(a, l, grid_rank):
    nd = a.ndim - 1
    return pl.BlockSpec((None,) + a.shape[1:], lambda *_: (l,) + (0,) * nd, pipeline_mode=pl.Buffered(1))


def _shared_spec(a):
    nd = a.ndim
    return pl.BlockSpec(a.shape, lambda *_: (0,) * nd, pipeline_mode=pl.Buffered(1))


def _head_meansq(x, e01, width):
    return _dot_x_exact01(x * x, e01) * (1.0 / width)


def _mixers_kernel(*refs, tl, sample, n_carry, tiles_per_seq):
    x_ref, sh_ref, sc_ref, gain_ref, w_in = refs[:5]
    (psm, ssd_cw, ssd_cb, ssd_nrm, s5_b, s5_ct, s5_l1, s5_p, s5_d, glu_w, s5_nrm,
     gdn_cw, gdn_nrm, gla_w2, gla_bg, gla_nrm, e01) = refs[5:22]
    n_in = 22
    if sample:
        st_ssd, hist_ssd, st_s5r, st_s5i, st_gdn, hist_gdn, st_gla, put3, get3, get1 = refs[22:32]
        n_in = 32 + n_carry
    outs = refs[n_in:n_in + 8]
    ymix, o_ssd, o_ssd_cv, o_s5r, o_s5i, o_gdn, o_gdn_cv, o_gla = outs
    (ext_ssd, ext_gdn, s_xbc, s_dt, s_la, s_beta, s_qkv, s_lg, s_y, s_hre, s_him, w_scr,
     zin, znext) = refs[n_in + 8:]

    step = pl.program_id(0)

    @pl.when(step == 0)
    def _():
        znext[...] = jnp.zeros(znext.shape, F32)

    rows = lax.broadcasted_iota(jnp.int32, (tl, 1), 0)
    valid = (rows & (SLOT - 1)) >= SLOT_OFF if sample else None
    first = None if sample else (lax.rem(jnp.maximum(step - 1, 0), tiles_per_seq) == 0)

    def masked(x):
        return jnp.where(valid, x, 0.0) if sample else x

    if sample:
        sh, sc = _expand_rows(sh_ref[...], SLOT), _expand_rows(sc_ref[...], SLOT)
    else:
        sh, sc = sh_ref[0], sc_ref[0]
    h_next = (_rms(x_ref[...], gain_ref[...]) * (1.0 + sc) + sh).astype(BF16)
    pending = list(range(0, NZ, PROJ_COLS))

    def fill(n=1):
        for _ in range(min(n, len(pending))):
            c0 = pending.pop(0)
            cols = slice(c0, min(c0 + PROJ_COLS, NZ))
            znext[:, cols] = jnp.dot(h_next, w_in[:, cols], preferred_element_type=F32)

    zin[:, 0:2 * PROJ_COLS] = znext[:, 0:2 * PROJ_COLS]
    fill(2)
    zin[:, 2 * PROJ_COLS:NZ] = znext[:, 2 * PROJ_COLS:NZ]

    if sample:
        ext_ssd[0:SUBLANES, :] = jnp.zeros((SUBLANES, SSD_CONV_DIM), F32)
        ext_gdn[0:SUBLANES, :] = jnp.zeros((SUBLANES, GDN_CONV_DIM), F32)
    else:
        @pl.when(first)
        def _():
            ext_ssd[0:SUBLANES, :] = jnp.zeros((SUBLANES, SSD_CONV_DIM), F32)
            ext_gdn[0:SUBLANES, :] = jnp.zeros((SUBLANES, GDN_CONV_DIM), F32)
            o_ssd[...] = jnp.zeros(o_ssd.shape, F32)
            o_gdn[...] = jnp.zeros(o_gdn.shape, F32)
            w_scr[...] = jnp.zeros(w_scr.shape, F32)
            o_s5r[...] = jnp.zeros(o_s5r.shape, F32)
            o_s5i[...] = jnp.zeros(o_s5i.shape, F32)

    def conv_in(ext_ref, col, hist_ref, w_ref, o_cv, finish):
        xin = zin[:, col:col + 768]
        if sample:
            hist = hist_ref[...]
            xin = jnp.where(valid, xin, _place_rows(put3[...], hist.reshape(-1, hist.shape[-1])))
            o_cv[...] = _place_rows(get3[...], xin).reshape(o_cv.shape)
        ext_ref[pl.ds(SUBLANES, tl), :] = xin
        for c0 in range(0, 768, 256):
            x, prev8 = ext_ref[pl.ds(SUBLANES, tl), c0:c0 + 256], ext_ref[0:SUBLANES, c0:c0 + 256]
            acc = w_ref[CONV_W - 1:CONV_W, c0:c0 + 256] * x
            for k in range(CONV_W - 1):
                acc = acc + w_ref[k:k + 1, c0:c0 + 256] * _shift_rows(x, prev8, CONV_W - 1 - k)
            finish(c0, acc)
            fill()
        if not sample:
            o_cv[0] = ext_ref[pl.ds(tl + SUBLANES - (CONV_W - 1), CONV_W - 1), :]
            ext_ref[0:SUBLANES, :] = ext_ref[pl.ds(tl, SUBLANES), :]

    def finish_ssd(c0, y):
        s_xbc[:, c0:c0 + 256] = _silu(y + ssd_cb[:, c0:c0 + 256])

    def finish_gdn(c0, y):
        y = _silu(y)
        if c0 < 512:
            inv = lax.rsqrt(_dot_x_exact01(y * y, e01[0:256, 0:256]) + EPS)
            y = y * inv * (GDN_HEADDIM ** -0.5) if c0 == 0 else y * inv
        s_qkv[:, c0:c0 + 256] = y

    fill()
    conv_in(ext_ssd, Z_XBC, hist_ssd if sample else None, ssd_cw, o_ssd_cv, finish_ssd)

    sm = zin[:, Z_SM:Z_SM + 128]
    sp = _softplus(sm + psm[0:1, :])
    s_dt[...] = masked(sp)
    s_la[...] = masked(-jnp.exp(psm[1:2, :]) * sp)
    s_beta[...] = masked(_sigmoid(sm))
    s_lg[...] = masked(-_softplus(-(_dot(sm, gla_w2[...]) + gla_bg[...])) * (1.0 / GLA_TAU))
    fill()

    conv_in(ext_gdn, Z_QKV, hist_gdn if sample else None, gdn_cw, o_gdn_cv, finish_gdn)

    u = zin[:, Z_U:Z_U + 256]
    n8 = tl // SUBLANES
    y5_parts = []
    side_steps = []

    def s5_scan(c0):
        cs_ = slice(c0, c0 + 256)
        x_re = masked(_dot(u, s5_b[0, :, cs_])).reshape(n8, SUBLANES, 256)
        x_im = masked(_dot(u, s5_b[1, :, cs_])).reshape(n8, SUBLANES, 256)
        for i, d in enumerate((1, 2, 4)):
            sr, si = pltpu.roll(x_re, d, 1), pltpu.roll(x_im, d, 1)
            ar, ai = s5_l1[i, :, cs_][None], s5_l1[3 + i, :, cs_][None]
            x_re, x_im = x_re + ar * sr - ai * si, x_im + ar * si + ai * sr
        if sample:
            c_re, c_im = st_s5r[:, cs_][:, None, :], st_s5i[:, cs_][:, None, :]
            pr, pi = s5_p[0, :, cs_][None], s5_p[1, :, cs_][None]
            x_re, x_im = x_re + pr * c_re - pi * c_im, x_im + pr * c_im + pi * c_re
        s_hre[:, cs_] = x_re.reshape(tl, 256)
        s_him[:, cs_] = x_im.reshape(tl, 256)

    def s5_carry(c0):
        cs_ = slice(c0, c0 + 256)
        pr, pi = s5_p[0, :, cs_], s5_p[1, :, cs_]
        c_re, c_im = o_s5r[0, :, cs_], o_s5i[0, :, cs_]
        for j in range(n8):
            rs = pl.ds(j * SUBLANES, SUBLANES)
            hr = s_hre[rs, cs_] + pr * c_re - pi * c_im
            hi = s_him[rs, cs_] + pr * c_im + pi * c_re
            s_hre[rs, cs_] = hr
            s_him[rs, cs_] = hi
            c_re, c_im = hr[SUBLANES - 1:SUBLANES, :], hi[SUBLANES - 1:SUBLANES, :]
        o_s5r[0, :, cs_] = c_re
        o_s5i[0, :, cs_] = c_im

    def s5_out(c0):
        cs_ = slice(c0, c0 + 256)
        y5_parts.append(_dot_nt(s_hre[:, cs_], s5_ct[0, :, cs_]) - _dot_nt(s_him[:, cs_], s5_ct[1, :, cs_]))

    def s5_finish():
        if sample:
            o_s5r[...] = _place_rows(get1[...], s_hre[...])
            o_s5i[...] = _place_rows(get1[...], s_him[...])
        y5 = (y5_parts[0] + y5_parts[1]) + (y5_parts[2] + y5_parts[3]) + s5_d[...] * u
        yy = _dot(y5, glu_w[...])
        ymix[:, 256:512] = masked(_rms(yy[:, 0:256] * _sigmoid(yy[:, 256:512]), s5_nrm[...]))

    for c0 in range(0, S5_WIDTH, 256):
        side_steps.append(functools.partial(s5_scan, c0))
        if not sample:
            side_steps.append(functools.partial(s5_carry, c0))
        side_steps.append(functools.partial(s5_out, c0))
    side_steps.append(s5_finish)

    def side(n=1):
        for _ in range(min(n, len(side_steps))):
            side_steps.pop(0)()
            fill()

    cs = SLOT if sample else GLA_CHUNK
    n_sub, sh = tl // cs, cs.bit_length() - 1
    wide = n_sub * 128
    gla = {}

    def expand_sub(x):
        rr_ = lax.broadcasted_iota(jnp.int32, (tl, wide), 0) >> sh
        cc_ = lax.broadcasted_iota(jnp.int32, (tl, wide), 1) >> 7
        return jnp.where(rr_ == cc_, jnp.concatenate([x] * n_sub, axis=1), 0.0)

    def gla_decays():
        rt = lax.broadcasted_iota(jnp.int32, (tl, tl), 0)
        ct = lax.broadcasted_iota(jnp.int32, (tl, tl), 1)
        sub_same = (rt >> sh) == (ct >> sh)
        m_cum = sub_same & (ct <= rt)
        m_ref = sub_same & ((ct & (cs - 1)) < cs // 2)
        m3 = jnp.concatenate([m_cum.astype(BF16), m_ref.astype(BF16), sub_same.astype(BF16)], axis=0)
        lg = s_lg[...]
        c3 = _dot_exact01(m3, lg)
        cumg, refg, clg = c3[0:tl], c3[tl:2 * tl], c3[2 * tl:3 * tl]
        gla_q = zin[:, Z_Q:Z_Q + 128] * (GLA_DK ** -0.5)
        gla_k = masked(zin[:, Z_K:Z_K + 128])
        gla.update(m_cum=m_cum, lg=lg, v=zin[:, Z_V:Z_V + 256],
                   qe=gla_q * jnp.exp(cumg - refg), ke=gla_k * jnp.exp(refg - cumg),
                   qd=gla_q * jnp.exp(cumg), kt=gla_k * jnp.exp(clg - cumg))

    def gla_intra(h):
        klane = lax.broadcasted_iota(jnp.int32, (1, 128), 1) >> 5
        vlane = lax.broadcasted_iota(jnp.int32, (1, 256), 1) >> 6
        sc = jnp.where(gla["m_cum"], _dot_nt(jnp.where(klane == h, gla["qe"], 0.0), gla["ke"]), 0.0)
        part = jnp.where(vlane == h, _dot(sc, gla["v"]), 0.0)
        gla["o"] = part if h == 0 else gla["o"] + part

    def gla_updates():
        head_diag = (((lax.broadcasted_iota(jnp.int32, (wide, 256), 0) >> 5) & 3)
                     == (lax.broadcasted_iota(jnp.int32, (wide, 256), 1) >> 6))
        msub = ((lax.broadcasted_iota(jnp.int32, (tl, n_sub), 0) >> sh)
                == lax.broadcasted_iota(jnp.int32, (tl, n_sub), 1)).astype(BF16)
        gla.update(head_diag=head_diag,
                   u_all=jnp.where(head_diag, _dot_tn(expand_sub(gla["kt"]), gla["v"]), 0.0),
                   dcol=jnp.exp(_dot_exact01_tn(gla["lg"], msub)))

    def gla_state():
        u_all, dcol = gla["u_all"], gla["dcol"]
        if sample:
            w2 = st_gla[...].reshape(n_sub * 128, GLA_DV)
            e4 = ((lax.broadcasted_iota(jnp.int32, (GLA_DV, 256), 1) & (GLA_DV - 1))
                  == lax.broadcasted_iota(jnp.int32, (GLA_DV, 256), 0)).astype(BF16)
            w_all = jnp.where(gla["head_diag"], _dot_x_exact01(w2, e4), 0.0)
        else:
            ws = []
            w = w_scr[...]
            for i in range(n_sub):
                ws.append(w)
                w = w * dcol[:, i:i + 1] + u_all[i * 128:(i + 1) * 128, :]
            w_scr[...] = w
            w_all = jnp.concatenate(ws, axis=0)
            for h in range(GLA_HEADS):
                o_gla[0, h] = w[h * GLA_DK:(h + 1) * GLA_DK, h * GLA_DV:(h + 1) * GLA_DV]
        s_y[:, 768:1024] = gla["o"] + _dot(expand_sub(gla["qd"]), w_all)
        if sample:
            for i in range(n_sub):
                w = w_all[i * 128:(i + 1) * 128, :] * dcol[:, i:i + 1] + u_all[i * 128:(i + 1) * 128, :]
                for h in range(GLA_HEADS):
                    o_gla[i, h] = w[h * GLA_DK:(h + 1) * GLA_DK, h * GLA_DV:(h + 1) * GLA_DV]

    side_steps += [gla_decays] + [functools.partial(gla_intra, h) for h in range(GLA_HEADS)] + [gla_updates, gla_state]

    c = SCAN_CHUNK
    groups = range(tl // c)
    heads = range(4)
    pairs = [(i, h) for i in groups for h in heads]
    spg = c // SLOT
    r = lax.broadcasted_iota(jnp.int32, (c, c), 0)
    s = lax.broadcasted_iota(jnp.int32, (c, c), 1)
    if sample:
        same = (r >> 3) == (s >> 3)
        lower, strict, upper = (s <= r) & same, (s < r) & same, (r <= s) & same
    else:
        lower, strict, upper = s <= r, s < r, r <= s
    lower01, upper01 = lower.astype(BF16), upper.astype(BF16)
    grp_rows = [pl.ds(i * c, c) for i in groups]

    def expand(x):
        n = x.shape[1]
        rr = lax.broadcasted_iota(jnp.int32, (c, spg * n), 0) >> 3
        cc = lax.broadcasted_iota(jnp.int32, (c, spg * n), 1) >> (n.bit_length() - 1)
        return jnp.where(rr == cc, jnp.concatenate([x] * spg, axis=1), 0.0)

    la = [s_la[rs, :] for rs in grp_rows]
    cum = [_dot_exact01(lower01, x) for x in la]
    side()
    cum_t = [_dot_exact01_tn(x, upper01) for x in la]
    side()
    if sample:
        same01 = same.astype(BF16)
        cl = [_dot_exact01(same01, x) for x in la]
    else:
        cl = [x[c - 1:c, :] for x in cum]
    tail = [jnp.exp(a_ - b_) for a_, b_ in zip(cl, cum)]
    ecum = [jnp.exp(x) for x in cum]
    ecl = [jnp.exp(x) for x in cl]

    def col(x, ln):
        return x[:, ln:ln + 1]

    def dec_of(i, ln):
        return jnp.where(lower, jnp.exp(col(cum[i], ln) - cum_t[i][ln:ln + 1, :]), 0.0)

    xbc = [s_xbc[rs, :] for rs in grp_rows]
    dt = [s_dt[rs, :] for rs in grp_rows]
    bgs = [[xbc[i][:, 256 + g * SSD_STATE:256 + (g + 1) * SSD_STATE] for g in range(2)] for i in groups]
    cgs = [[xbc[i][:, 512 + g * SSD_STATE:512 + (g + 1) * SSD_STATE] for g in range(2)] for i in groups]
    cb = [[_dot_nt(cgs[i][g], bgs[i][g]) for g in range(2)] for i in groups]
    side()
    xh = {(i, h): xbc[i][:, h * SSD_HEADDIM:(h + 1) * SSD_HEADDIM] for i, h in pairs}
    xq = {p: xh[p] * col(dt[p[0]], SM_DT + p[1]) for p in pairs}
    y_ssd = {(i, h): _dot(cb[i][h // 2] * dec_of(i, SM_DT + h), xq[i, h]) + psm[2:3, h:h + 1] * xh[i, h]
             for i, h in pairs}
    side()
    bt = {(i, h): bgs[i][h // 2] * col(tail[i], SM_DT + h) for i, h in pairs}
    cq = {(i, h): cgs[i][h // 2] * col(ecum[i], SM_DT + h) for i, h in pairs}
    upd = {p: _dot_tn(xq[p], expand(bt[p]) if sample else bt[p]) for p in pairs}

    side()
    qkv = [s_qkv[rs, :] for rs in grp_rows]
    beta = [s_beta[rs, :] for rs in grp_rows]
    hd = GDN_HEADDIM
    gq_ = {(i, h): qkv[i][:, h * hd:(h + 1) * hd] for i, h in pairs}
    gk = {(i, h): qkv[i][:, 256 + h * hd:256 + (h + 1) * hd] for i, h in pairs}
    gv = {(i, h): qkv[i][:, 512 + h * hd:512 + (h + 1) * hd] for i, h in pairs}
    gdec = {(i, h): dec_of(i, SM_GA + h) for i, h in pairs}
    gb = {(i, h): col(beta[i], SM_GB + h) for i, h in pairs}
    kk = {p: _dot_nt(gk[p], gk[p]) for p in pairs}
    side()
    qk_d = {p: _dot_nt(gq_[p], gk[p]) * gdec[p] for p in pairs}
    side()
    a_mat = {p: jnp.where(strict, gdec[p] * kk[p], 0.0) * gb[p] for p in pairs}
    t_inv = dict(zip(pairs, _unit_lower_inverse([a_mat[p] for p in pairs], c,
                                                blk=SLOT if sample else 2 * SUBLANES, merge=not sample,
                                                between=side)))
    ge = {(i, h): col(ecum[i], SM_GA + h) for i, h in pairs}
    rhs = {p: jnp.concatenate([gv[p] * gb[p], gk[p] * (gb[p] * ge[p])], axis=1) for p in pairs}
    x = {p: _dot(t_inv[p], rhs[p]) for p in pairs}
    side()
    ktl = {(i, h): gk[i, h] * col(tail[i], SM_GA + h) for i, h in pairs}
    gqe = {p: gq_[p] * ge[p] for p in pairs}

    if sample:
        slots = [pl.ds(i * spg, spg) for i in groups]
        last = [ecl[i].reshape(spg, SLOT, 128)[:, SLOT - 1:SLOT, :] for i in groups]
        s_all = {(i, h): st_gdn[slots[i], h] for i, h in pairs}
        m = {p: jnp.concatenate([expand(x[p][:, hd:2 * hd]), expand(gqe[p])], axis=0) for p in pairs}
        rr = {p: _dot(m[p], s_all[p].reshape(spg * hd, hd)) for p in pairs}
        delta = {p: x[p][:, 0:hd] - rr[p][0:c] for p in pairs}
        og = {p: rr[p][c:2 * c] + _dot(qk_d[p], delta[p]) for p in pairs}
        un = {p: _dot_tn(expand(ktl[p]), delta[p]) for p in pairs}
        h_all = {(i, h): [st_ssd[i * spg + g, h] for g in range(spg)] for i, h in pairs}
        yi = {p: _dot_nt(expand(cq[p]), jnp.concatenate(h_all[p], axis=1)) for p in pairs}
        for i, h in pairs:
            rs = grp_rows[i]
            o_gdn[slots[i], h] = (s_all[i, h] * last[i][:, :, SM_GA + h:SM_GA + h + 1]
                                  + un[i, h].reshape(spg, hd, hd))
            s_y[rs, 512 + h * hd:512 + (h + 1) * hd] = og[i, h]
            for g in range(spg):
                o_ssd[i * spg + g, h] = (h_all[i, h][g] * last[i][g, :, SM_DT + h:SM_DT + h + 1]
                                         + upd[i, h][:, g * SSD_STATE:(g + 1) * SSD_STATE])
            s_y[rs, h * SSD_HEADDIM:(h + 1) * SSD_HEADDIM] = y_ssd[i, h] + yi[i, h]
    else:
        fg = {p: _dot_tn(ktl[p], x[p]) for p in pairs}
        side()
        qo = {p: _dot(qk_d[p], x[p]) for p in pairs}
        side()
        for i in groups:
            rs = grp_rows[i]
            s0 = [o_gdn[0, h] for h in heads]
            h0 = [o_ssd[0, h] for h in heads]
            m = [jnp.concatenate([gqe[i, h] - qo[i, h][:, hd:2 * hd], fg[i, h][:, hd:2 * hd]], axis=0) for h in heads]
            rr = [_dot(m[h], s0[h]) for h in heads]
            yi = [_dot_nt(cq[i, h], h0[h]) for h in heads]
            for h in heads:
                o_gdn[0, h] = s0[h] * col(ecl[i], SM_GA + h) - rr[h][c:c + hd] + fg[i, h][:, 0:hd]
                s_y[rs, 512 + h * hd:512 + (h + 1) * hd] = rr[h][0:c] + qo[i, h][:, 0:hd]
                o_ssd[0, h] = h0[h] * col(ecl[i], SM_DT + h) + upd[i, h]
                s_y[rs, h * SSD_HEADDIM:(h + 1) * SSD_HEADDIM] = y_ssd[i, h] + yi[h]
            side()

    side(len(side_steps))

    fill(len(pending))

    e256 = e01[0:256, 0:256]
    y = s_y[:, 0:256] * _silu(zin[:, Z_Z:Z_Z + 256])
    ymix[:, 0:256] = masked(_rms(y, ssd_nrm[...]))
    o = s_y[:, 512:768]
    o = o * lax.rsqrt(_head_meansq(o, e256, GDN_HEADDIM) + EPS) * gdn_nrm[...]
    ymix[:, 512:768] = masked(o * _silu(zin[:, Z_GZ:Z_GZ + 256]))
    o = s_y[:, 768:1024]
    o = o * lax.rsqrt(_head_meansq(o, e256, GLA_DV) + EPS) * gla_nrm[...]
    ymix[:, 768:1024] = masked(o * _silu(zin[:, Z_GG:Z_GG + 256]))


def _slot_move_matrices(tl, rows):
    slots = tl // SLOT
    put = np.zeros((tl, rows * slots), np.float32)
    get = np.zeros((rows * slots, tl), np.float32)
    for g in range(slots):
        for j in range(rows):
            put[g * SLOT + SLOT_OFF - rows + j, j * slots + g] = 1.0
            get[j * slots + g, g * SLOT + SLOT - rows + j] = 1.0
    return jnp.asarray(put, BF16), jnp.asarray(get, BF16)


def _mixers(x2d, mod, l, prm, *, batch, rows_per_seq, sample, states=None, carry=None):
    t = x2d.shape[0]
    tl = min(SAMPLE_MIX_TILE if sample else TILE_TOKENS, t)
    n_tiles = t // tl
    nseq = t // SLOT if sample else batch
    nrow = CONV_W - 1
    dims = STATE_DIMS
    grid = (n_tiles + 1,)
    nxt = lambda g: jnp.minimum(g, n_tiles - 1)
    cur = lambda g: jnp.maximum(g - 1, 0)
    tok = lambda w: pl.BlockSpec((tl, w), lambda g: (cur(g), 0))
    if sample:
        slots = tl // SLOT
        tiles = 1
        st_spec = lambda d: pl.BlockSpec((None, slots) + d, lambda g: (l, cur(g), 0, 0, 0))
        st_shape = lambda d: jax.ShapeDtypeStruct((DEPTH, nseq) + d, F32)
        cv_in = lambda w: pl.BlockSpec((None, nrow, slots, w), lambda g: (l, 0, cur(g), 0))
        row_in = lambda w: pl.BlockSpec((None, slots, w), lambda g: (l, cur(g), 0))
        cv_spec = pl.BlockSpec((nrow, slots, 768), lambda g: (0, cur(g), 0))
        cv_shape = jax.ShapeDtypeStruct((nrow, nseq, 768), F32)
        s5_spec = pl.BlockSpec((slots, S5_WIDTH), lambda g: (cur(g), 0))
        s5_shape = jax.ShapeDtypeStruct((nseq, S5_WIDTH), F32)
        mspecs = [pl.BlockSpec((None, slots, D_MODEL), (lambda g, j=j: (l, nxt(g), j))) for j in (0, 1)]
        mops = [mod] * 2
    else:
        tiles = rows_per_seq // tl
        seq_of = lambda g: cur(g) // tiles
        st_spec = lambda d: pl.BlockSpec((1,) + d, lambda g: (seq_of(g), 0, 0, 0))
        st_shape = lambda d: jax.ShapeDtypeStruct((batch,) + d, F32)
        cv_spec = pl.BlockSpec((1, nrow, 768), lambda g: (seq_of(g), 0, 0))
        cv_shape = jax.ShapeDtypeStruct((batch, nrow, 768), F32)
        s5_spec = pl.BlockSpec((1, 1, S5_WIDTH), lambda g: (seq_of(g), 0, 0))
        s5_shape = jax.ShapeDtypeStruct((batch, 1, S5_WIDTH), F32)
        m3 = mod.reshape(DEPTH * batch * 6, 1, D_MODEL)
        mspecs = [pl.BlockSpec((1, 1, D_MODEL), (lambda g, j=j: ((l * batch + nxt(g) // tiles) * 6 + j, 0, 0)))
                  for j in (0, 1)]
        mops = [m3] * 2
    names = ["psmall", "ssd_cw", "ssd_cb", "ssd_nrm", "s5_b", "s5_ct", "s5_l1", "s5_ps" if sample else "s5_pp",
             "s5_d", "glu_w", "s5_nrm", "gdn_cw", "gdn_nrm", "gla_w2", "gla_bg", "gla_nrm"]
    consts = [prm["norm_mix"], prm["w_in"]] + [prm[n] for n in names]
    in_specs = ([pl.BlockSpec((tl, D_MODEL), lambda g: (nxt(g), 0))] + mspecs
                + [_layer_spec(a, l, 1) for a in consts] + [_shared_spec(prm["e01"])])
    operands = [x2d] + mops + consts + [prm["e01"]]
    aliases = {}
    if sample:
        put3, get3 = _slot_move_matrices(tl, nrow)
        _, get1 = _slot_move_matrices(tl, 1)
        in_specs += [st_spec(dims[0]), cv_in(SSD_CONV_DIM), row_in(S5_WIDTH), row_in(S5_WIDTH),
                     st_spec(dims[1]), cv_in(GDN_CONV_DIM), st_spec(dims[2]),
                     _shared_spec(put3), _shared_spec(get3), _shared_spec(get1)]
        operands += list(states) + [put3, get3, get1]
        if carry is not None:
            aliases = {len(operands) + k: out_idx for k, out_idx in enumerate((1, 5, 7))}
            in_specs += [pl.BlockSpec(memory_space=pl.ANY)] * 3
            operands += list(carry)
    out_specs = [tok(D_MODEL), st_spec(dims[0]), cv_spec, s5_spec, s5_spec, st_spec(dims[1]), cv_spec,
                 st_spec(dims[2])]
    out_shape = [jax.ShapeDtypeStruct((t, D_MODEL), F32), st_shape(dims[0]), cv_shape, s5_shape, s5_shape,
                 st_shape(dims[1]), cv_shape, st_shape(dims[2])]
    scratch = [pltpu.VMEM((tl + SUBLANES, 768), F32), pltpu.VMEM((tl + SUBLANES, 768), F32),
               pltpu.VMEM((tl, 768), F32), pltpu.VMEM((tl, 128), F32), pltpu.VMEM((tl, 128), F32),
               pltpu.VMEM((tl, 128), F32), pltpu.VMEM((tl, 768), F32), pltpu.VMEM((tl, 128), F32),
               pltpu.VMEM((tl, D_MODEL), F32), pltpu.VMEM((tl, S5_WIDTH), F32), pltpu.VMEM((tl, S5_WIDTH), F32),
               pltpu.VMEM((GLA_HEADS * GLA_DK, GLA_HEADS * GLA_DV), F32), pltpu.VMEM((tl, NZ), F32),
               pltpu.VMEM((tl, NZ), F32)]
    return pl.pallas_call(
        functools.partial(_mixers_kernel, tl=tl, sample=sample, n_carry=len(aliases), tiles_per_seq=tiles),
        grid=grid, in_specs=in_specs, out_specs=out_specs, out_shape=out_shape, scratch_shapes=scratch,
        input_output_aliases=aliases,
        compiler_params=pltpu.CompilerParams(dimension_semantics=("arbitrary",) * len(grid),
                                             vmem_limit_bytes=VMEM_LIMIT),
        name="mixers_sample" if sample else "mixers_prompt",
    )(*operands)


def _outffn_kernel(*refs, tm, sample, final):
    x_ref, y_ref, g1_ref, sh_ref, sc_ref, g2_ref, nf_ref, wout, up, cw, cb, down = refs[:12]
    k = 12
    hist = put2 = get2 = fin = None
    if sample:
        hist, put2, get2 = refs[k:k + 3]
        k += 3
    if final:
        fin = refs[k]
        k += 1
    x2_ref, cv_ref, ext = refs[k], refs[k + 1], refs[k + 2]

    if sample:
        ex = lambda r: _expand_rows(r[...], SLOT)
        g1, sh, sc, g2 = ex(g1_ref), ex(sh_ref), ex(sc_ref), ex(g2_ref)
        valid = (lax.broadcasted_iota(jnp.int32, (tm, 1), 0) & (SLOT - 1)) >= SLOT_OFF
        ext[0:SUBLANES, :] = jnp.zeros((SUBLANES, 2 * D_FF), F32)
    else:
        g1, sh, sc, g2 = g1_ref[0], sh_ref[0], sc_ref[0], g2_ref[0]

        @pl.when(pl.program_id(1) == 0)
        def _():
            ext[0:SUBLANES, :] = jnp.zeros((SUBLANES, 2 * D_FF), F32)

    x1 = x_ref[...] + g1 * _dot(y_ref[...], wout[...])
    h = (_rms(x1, nf_ref[...]) * (1.0 + sc) + sh).astype(BF16)
    def up_proj(j):
        uj = jnp.dot(h, up[:, j:j + FF_CHUNK], preferred_element_type=F32)
        if sample:
            hj = jnp.concatenate([hist[:, r, j:j + FF_CHUNK] for r in range(FFN_CONV_W - 1)], axis=0)
            uj = jnp.where(valid, uj, _place_rows(put2[...], hj))
        ext[pl.ds(SUBLANES, tm), j:j + FF_CHUNK] = uj

    def conv(col):
        cols = slice(col, col + FF_CHUNK)
        x, prev8 = ext[pl.ds(SUBLANES, tm), cols], ext[0:SUBLANES, cols]
        acc = cb[:, cols] + cw[FFN_CONV_W - 1:FFN_CONV_W, cols] * x
        for kk in range(FFN_CONV_W - 1):
            acc = acc + cw[kk:kk + 1, cols] * _shift_rows(x, prev8, FFN_CONV_W - 1 - kk)
        return acc

    for j in range(0, 2 * D_FF, FF_CHUNK):
        up_proj(j)
    f = None
    for j in range(0, D_FF, FF_CHUNK):
        part = _dot(_silu(conv(j)) * conv(D_FF + j), down[j:j + FF_CHUNK, :])
        f = part if f is None else f + part
    x2 = x1 + g2 * f
    if final:
        x2 = _rms(x2, fin[...])
    if sample:
        x2_ref[...] = jnp.where(valid, x2, 0.0)
        for j in range(0, 2 * D_FF, FF_CHUNK):
            got = _place_rows(get2[...], ext[pl.ds(SUBLANES, tm), j:j + FF_CHUNK])
            for r in range(FFN_CONV_W - 1):
                cv_ref[:, r, j:j + FF_CHUNK] = got[r * (tm // SLOT):(r + 1) * (tm // SLOT)]
    else:
        x2_ref[...] = x2
        cv_ref[0] = ext[pl.ds(tm + SUBLANES - (FFN_CONV_W - 1), FFN_CONV_W - 1), :]
        ext[0:SUBLANES, :] = ext[pl.ds(tm, SUBLANES), :]


def _out_ffn(x2d, ymix, mod, l, prm, *, batch, rows_per_seq, sample, hist=None, final_gain=None):
    t = x2d.shape[0]
    tm = min(SAMPLE_FFN_TILE if sample else FFN_TILE, t)
    final = final_gain is not None
    nrows = FFN_CONV_W - 1
    if sample:
        slots = tm // SLOT
        grid = (t // tm,)
        tok = lambda w: pl.BlockSpec((tm, w), lambda i: (i, 0))
        mspecs = [pl.BlockSpec((None, slots, D_MODEL), (lambda i, j=j: (l, i, j))) for j in (2, 3, 4, 5)]
        mops = [mod] * 4
        cv_spec = pl.BlockSpec((slots, nrows, 2 * D_FF), lambda i: (i, 0, 0))
        cv_shape = jax.ShapeDtypeStruct((t // SLOT, nrows, 2 * D_FF), F32)
    else:
        tiles = rows_per_seq // tm
        grid = (batch, tiles)
        tok = lambda w: pl.BlockSpec((tm, w), lambda b, j: (b * tiles + j, 0))
        m3 = mod.reshape(DEPTH * batch * 6, 1, D_MODEL)
        mspecs = [pl.BlockSpec((1, 1, D_MODEL), (lambda b, i, j=j: ((l * batch + b) * 6 + j, 0, 0))) for j in (2, 3, 4, 5)]
        mops = [m3] * 4
        cv_spec = pl.BlockSpec((1, nrows, 2 * D_FF), lambda b, j: (b, 0, 0))
        cv_shape = jax.ShapeDtypeStruct((batch, nrows, 2 * D_FF), F32)
    consts = [prm[n] for n in ("norm_ffn", "w_out", "ffn_up", "ffn_cw", "ffn_cb", "ffn_down")]
    in_specs = [tok(D_MODEL), tok(D_MODEL)] + mspecs + [_layer_spec(a, l, len(grid)) for a in consts]
    operands = [x2d, ymix] + mops + consts
    if sample:
        put2, get2 = _slot_move_matrices(tm, nrows)
        in_specs += [pl.BlockSpec((None, slots, nrows, 2 * D_FF), lambda i: (l, i, 0, 0)),
                     _shared_spec(put2), _shared_spec(get2)]
        operands += [hist, put2, get2]
    if final:
        in_specs.append(_shared_spec(final_gain))
        operands.append(final_gain)
    return pl.pallas_call(
        functools.partial(_outffn_kernel, tm=tm, sample=sample, final=final),
        grid=grid, in_specs=in_specs,
        out_specs=[tok(D_MODEL), cv_spec],
        out_shape=[jax.ShapeDtypeStruct((t, D_MODEL), F32), cv_shape],
        scratch_shapes=[pltpu.VMEM((tm + SUBLANES, 2 * D_FF), F32)],
        compiler_params=pltpu.CompilerParams(dimension_semantics=("arbitrary",) * len(grid),
                                             vmem_limit_bytes=VMEM_LIMIT),
        name="out_ffn_sample" if sample else "out_ffn_prompt",
    )(*operands)


def _reorder_w_in(w_in):
    o = np.cumsum([0, 256, 768, 4, 256, 768, 256, 4, 4, 128, 128, 256, 256, 16])
    seg = lambda i: w_in[:, :, o[i]:o[i + 1]]
    small = [seg(2), seg(6), seg(7), seg(12)]
    pad = jnp.zeros(w_in.shape[:2] + (128 - 28,), w_in.dtype)
    cols = [seg(1), seg(4), seg(0), seg(3), seg(5), seg(10), seg(11), seg(8), seg(9)] + small + [pad]
    return jnp.concatenate(cols, axis=-1).astype(BF16)


def _pad_lanes(a, lane0, width=128):
    return jnp.pad(a, ((0, 0), (lane0, width - lane0 - a.shape[1])))[:, None, :]


def kernel(x_prompt, x_sample, c_prompt, c_sample, state_ssd, state_ssd_conv, state_s5_re, state_s5_im,
           state_gdn, state_gdn_conv, state_gla, state_ffn_conv, ada_w, ada_b, norm_mix, norm_ffn, w_in, w_out,
           ssd_conv_w, ssd_conv_b, ssd_dt_bias, ssd_a_log, ssd_d, ssd_norm, s5_a_re, s5_a_im, s5_log_dt,
           s5_b_re, s5_b_im, s5_c_re, s5_c_im, s5_d, s5_glu_w, s5_norm, gdn_conv_w, gdn_a_log, gdn_dt_bias,
           gdn_norm, gla_wg2, gla_bg, gla_norm, ffn_up, ffn_conv_w, ffn_conv_b, ffn_down, final_norm):
    nb, seq = x_prompt.shape[0], x_prompt.shape[1]
    ns, dseq = x_sample.shape[0], x_sample.shape[1]
    assert dseq == SLOT - SLOT_OFF

    mod_p, mod_s = _modulation(c_prompt.astype(F32), c_sample.astype(F32), ada_w, ada_b)
    s5_b, s5_ct, s5_l1, s5_pp, s5_ps = _s5_prepare(s5_a_re, s5_a_im, s5_log_dt, s5_b_re, s5_b_im, s5_c_re, s5_c_im)

    row = lambda a: a[:, None, :]
    psmall = jnp.concatenate([
        _pad_lanes(ssd_dt_bias, SM_DT) + _pad_lanes(gdn_dt_bias, SM_GA),
        _pad_lanes(ssd_a_log, SM_DT) + _pad_lanes(gdn_a_log, SM_GA),
        _pad_lanes(ssd_d, 0), jnp.zeros((DEPTH, SUBLANES - 3, 128), F32)], axis=1)
    prm = dict(
        psmall=psmall, ssd_cw=ssd_conv_w, ssd_cb=row(ssd_conv_b), ssd_nrm=row(ssd_norm),
        s5_b=s5_b, s5_ct=s5_ct, s5_l1=s5_l1, s5_pp=s5_pp, s5_ps=s5_ps, s5_d=row(s5_d),
        glu_w=s5_glu_w.astype(BF16), s5_nrm=row(s5_norm), gdn_cw=gdn_conv_w,
        gdn_nrm=row(jnp.tile(gdn_norm, (1, GDN_HEADS))),
        gla_w2=jnp.pad(gla_wg2, ((0, 0), (SM_LR, 128 - SM_LR - GLA_GATE_RANK), (0, 0))).astype(BF16),
        gla_bg=row(gla_bg), gla_nrm=row(jnp.tile(gla_norm, (1, GLA_HEADS))),
        e01=jnp.asarray(np.arange(512)[:, None] // 64 == np.arange(512)[None, :] // 64, BF16),
        norm_mix=row(norm_mix), norm_ffn=row(norm_ffn), w_in=_reorder_w_in(w_in), w_out=w_out.astype(BF16),
        ffn_up=ffn_up.astype(BF16), ffn_cw=ffn_conv_w, ffn_cb=row(ffn_conv_b), ffn_down=ffn_down.astype(BF16))
    fin = final_norm.reshape(1, D_MODEL)

    xp = x_prompt.astype(F32).reshape(nb * seq, D_MODEL)
    p_new = []
    for l in range(DEPTH):
        ymix, h_ssd, cv_ssd, h5r, h5i, s_gdn, cv_gdn, s_gla = _mixers(xp, mod_p, l, prm, batch=nb, rows_per_seq=seq,
                                                                      sample=False)
        xp, cv_ffn = _out_ffn(xp, ymix, mod_p, l, prm, batch=nb, rows_per_seq=seq, sample=False,
                              final_gain=fin if l == DEPTH - 1 else None)
        p_new.append((jnp.swapaxes(h_ssd, 2, 3), cv_ssd, h5r.reshape(nb, S5_GROUPS, S5_STATE),
                      h5i.reshape(nb, S5_GROUPS, S5_STATE), s_gdn, cv_gdn, s_gla, cv_ffn))

    xs = jnp.pad(x_sample.astype(F32), ((0, 0), (SLOT_OFF, 0), (0, 0))).reshape(ns * SLOT, D_MODEL)
    f32 = lambda a: a.astype(F32)
    states = (jnp.swapaxes(f32(state_ssd), 3, 4), jnp.swapaxes(f32(state_ssd_conv), 1, 2),
              f32(state_s5_re).reshape(DEPTH, ns, S5_WIDTH), f32(state_s5_im).reshape(DEPTH, ns, S5_WIDTH),
              f32(state_gdn), jnp.swapaxes(f32(state_gdn_conv), 1, 2), f32(state_gla))
    hist = f32(state_ffn_conv)
    s_new, carry = [], tuple(jnp.zeros((DEPTH, ns) + d, F32) for d in STATE_DIMS)
    for l in range(DEPTH):
        ymix, h_ssd, cv_ssd, h5r, h5i, s_gdn, cv_gdn, s_gla = _mixers(xs, mod_s, l, prm, batch=ns, rows_per_seq=SLOT,
                                                                      sample=True, states=states, carry=carry)
        carry = (h_ssd, s_gdn, s_gla)
        xs, cv_ffn = _out_ffn(xs, ymix, mod_s, l, prm, batch=ns, rows_per_seq=SLOT, sample=True, hist=hist,
                              final_gain=fin if l == DEPTH - 1 else None)
        s_new.append((jnp.swapaxes(cv_ssd, 0, 1), h5r.reshape(ns, S5_GROUPS, S5_STATE),
                      h5i.reshape(ns, S5_GROUPS, S5_STATE), jnp.swapaxes(cv_gdn, 0, 1), cv_ffn))

    y_p = xp.reshape(nb, seq, D_MODEL).astype(x_prompt.dtype)
    y_s = xs.reshape(ns, SLOT, D_MODEL)[:, SLOT_OFF:, :].astype(x_sample.dtype)
    p_st = [jnp.stack(t) for t in zip(*p_new)]
    s_cv_ssd, s_5r, s_5i, s_cv_gdn, s_cv_ffn = [jnp.stack(t) for t in zip(*s_new)]
    s_ssd, s_gdn, s_gla = carry
    return (y_p, y_s, *p_st,
            jnp.swapaxes(s_ssd, 3, 4), s_cv_ssd, s_5r, s_5i, s_gdn, s_cv_gdn, s_gla, s_cv_ffn)
```
